```python
import math
import jax, jax.numpy as jnp
from jax import lax
import numpy as np

D_MODEL = 1024
BATCH = 8
SEQ = 2048
DEPTH = 2

N_MEM = 256
H_A = 4
D_A = 64
DV_A = 2 * D_A
Q_BLOCK = 128
ROPE_THETA = 500000.0
ROPE_DIM = D_A // 4
H_B = 4
DK_B = 64
DV_B = 128
GATE_RANK = 16
GATE_NORMALIZER = 16.0
H_C = 8
DK_C = 128
DV_C = 128
CONV_C = 4
CHUNK = 64
N_XA = 4
D_XA = D_MODEL // N_XA
D_FF = int(math.ceil(8 * D_MODEL / 3 / 128)) * 128
FFN_CONV = 3
LN_EPS = 1e-5
RMS_EPS = 1e-6
DEEPNORM_ALPHA = (2.0 * DEPTH) ** 0.25
DEEPNORM_BETA = (8.0 * DEPTH) ** -0.25
P0_SPLITS = (H_A * 2 * D_A, H_A * 2 * D_A, H_A * DV_A,
             H_B * DK_B, H_B * DK_B, H_B * DV_B, H_B * DV_B, GATE_RANK)
P1_SPLITS = (H_C * DK_C, H_C * DK_C, H_C * DV_C, H_C * DV_C, H_C, H_C)
P0 = sum(P0_SPLITS)
P1 = sum(P1_SPLITS)

kernel_name = 'hybrid_diffattn_gla_gdn_deepnorm'

F32 = jnp.float32


def _split(h, sizes):
    out, start = [], 0
    for s in sizes:
        out.append(h[..., start:start + s])
        start += s
    return out


def layer_norm(x, g, b):
    xf = x.astype(F32)
    xc = xf - jnp.mean(xf, axis=-1, keepdims=True)
    var = jnp.mean(xc * xc, axis=-1, keepdims=True)
    return (xc * lax.rsqrt(var + LN_EPS)).astype(x.dtype) * g + b


def rms_norm(x, g):
    xf = x.astype(F32)
    y = xf * lax.rsqrt(jnp.mean(xf * xf, axis=-1, keepdims=True) + RMS_EPS)
    return y * g.astype(F32)


def l2norm(x):
    return x * lax.rsqrt(jnp.sum(x * x, axis=-1, keepdims=True) + RMS_EPS)


def causal_dwconv(x, w, b=None):
    K = w.shape[0]
    S_ = x.shape[1]
    xp = jnp.pad(x, ((0, 0), (K - 1, 0), (0, 0)))
    y = xp[:, 0:S_] * w[0]
    for j in range(1, K):
        y = y + xp[:, j:j + S_] * w[j]
    if b is not None:
        y = y + b
    return y


def rope_cos_sin(pos):
    inv_freq = ROPE_THETA ** (-jnp.arange(0, ROPE_DIM, 2, dtype=F32) / ROPE_DIM)
    ang = pos.astype(F32)[..., None] * inv_freq
    return jnp.cos(ang), jnp.sin(ang)


def apply_partial_rope(x, cos, sin):
    half = ROPE_DIM // 2
    xf = x.astype(F32)
    x1 = xf[..., :half]
    x2 = xf[..., half:ROPE_DIM]
    rot = jnp.concatenate([x1 * cos - x2 * sin, x2 * cos + x1 * sin, xf[..., ROPE_DIM:]], axis=-1)
    return rot.astype(x.dtype)


def diff_attention(q, k, v, lam_vecs, norm_g, positions, layer_idx):
    B_, S_, _ = q.shape
    q = q.reshape(B_, S_, H_A, 2, D_A)
    k = k.reshape(B_, S_, H_A, 2, D_A)
    v = v.reshape(B_, S_, H_A, DV_A)
    cos, sin = rope_cos_sin(positions)
    cos = cos[:, :, None, None, :]
    sin = sin[:, :, None, None, :]
    q = apply_partial_rope(q, cos, sin) * (D_A ** -0.5)
    k = apply_partial_rope(k, cos, sin)
    lam_init = 0.8 - 0.6 * math.exp(-0.3 * layer_idx)
    lf = lam_vecs.astype(F32)
    lam = jnp.exp(jnp.sum(lf[0] * lf[1])) - jnp.exp(jnp.sum(lf[2] * lf[3])) + lam_init
    qh = q.transpose(0, 2, 3, 1, 4)
    kh = k.transpose(0, 2, 3, 1, 4)
    vh = v.transpose(0, 2, 1, 3)
    nb = S_ // Q_BLOCK
    qb = jnp.moveaxis(qh.reshape(B_, H_A, 2, nb, Q_BLOCK, D_A), 3, 0)
    kpos = jnp.arange(S_)

    def one_block(args):
        qi, bi = args
        s = jnp.einsum('bhmqd,bhmkd->bhmqk', qi, kh).astype(F32)
        qpos = bi * Q_BLOCK + jnp.arange(Q_BLOCK)
        s = jnp.where(kpos[None, :] <= qpos[:, None], s, -jnp.inf)
        p = jax.nn.softmax(s, axis=-1)
        a = p[:, :, 0] - lam * p[:, :, 1]
        return jnp.einsum('bhqk,bhkv->bhqv', a.astype(vh.dtype), vh)

    ob = lax.map(one_block, (qb, jnp.arange(nb)))
    o = jnp.moveaxis(ob, 0, 2).reshape(B_, H_A, S_, DV_A).transpose(0, 2, 1, 3)
    o = rms_norm(o, norm_g) * (1.0 - lam_init)
    return o.reshape(B_, S_, H_A * DV_A).astype(q.dtype)


def gla(q, k, v, g_low, r, w2, b2, norm_g):
    B_, S_, _ = q.shape
    N = S_ // CHUNK
    gk = jax.nn.log_sigmoid((g_low @ w2 + b2).astype(F32)) / GATE_NORMALIZER

    def heads(t, d):
        return t.astype(F32).reshape(B_, N, CHUNK, H_B, d).transpose(1, 0, 3, 2, 4)

    qc = heads(q, DK_B) * (DK_B ** -0.5)
    kc = heads(k, DK_B)
    vc = heads(v, DV_B)
    gc = heads(gk, DK_B)
    causal = jnp.tril(jnp.ones((CHUNK, CHUNK), dtype=bool))

    def step(state, inp):
        qi, ki, vi, gi = inp
        b = jnp.cumsum(gi, axis=2)
        inter = jnp.einsum('bhtk,bhkv->bhtv', qi * jnp.exp(b), state)
        diff = jnp.where(causal[:, :, None], b[:, :, :, None, :] - b[:, :, None, :, :], -jnp.inf)
        att = jnp.einsum('bhtk,bhsk,bhtsk->bhts', qi, ki, jnp.exp(diff))
        out = inter + jnp.einsum('bhts,bhsv->bhtv', att, vi)
        b_last = b[:, :, -1:, :]
        state = state * jnp.exp(b_last)[:, :, 0, :, None] + jnp.einsum('bhsk,bhsv->bhkv', ki * jnp.exp(b_last - b), vi)
        return state, out

    s0 = jnp.zeros((B_, H_B, DK_B, DV_B), F32)
    _, oc = lax.scan(step, s0, (qc, kc, vc, gc))
    o = oc.transpose(1, 0, 3, 2, 4).reshape(B_, S_, H_B, DV_B)
    o = rms_norm(o, norm_g) * jax.nn.silu(r.astype(F32)).reshape(B_, S_, H_B, DV_B)
    return o.reshape(B_, S_, H_B * DV_B).astype(q.dtype)


def gated_deltanet(q, k, v, beta_logit, a_logit, z, a_log, dt_bias, norm_g):
    B_, S_, _ = q.shape
    N = S_ // CHUNK

    def heads(t, d):
        return t.astype(F32).reshape(B_, N, CHUNK, H_C, d).transpose(0, 3, 1, 2, 4)

    def scal(t):
        return t.reshape(B_, N, CHUNK, H_C).transpose(0, 3, 1, 2)

    qh = l2norm(heads(q, DK_C)) * (DK_C ** -0.5)
    kh = l2norm(heads(k, DK_C))
    vh = heads(v, DV_C)
    beta = scal(jax.nn.sigmoid(beta_logit.astype(F32)))
    g = -jnp.exp(a_log.astype(F32)) * jax.nn.softplus(a_logit.astype(F32) + dt_bias.astype(F32))
    gc = jnp.cumsum(scal(g), axis=-1)
    incl = jnp.tril(jnp.ones((CHUNK, CHUNK), dtype=bool))
    strict = jnp.tril(jnp.ones((CHUNK, CHUNK), dtype=bool), k=-1)
    decay = jnp.exp(jnp.where(incl, gc[..., :, None] - gc[..., None, :], -jnp.inf))
    kb = kh * beta[..., None]
    m = jnp.where(strict, jnp.einsum('bhntk,bhnsk->bhnts', kb, kh) * decay, 0.0)
    eye = jnp.eye(CHUNK, dtype=F32)
    tinv = lax.linalg.triangular_solve(eye + m, jnp.broadcast_to(eye, m.shape), left_side=True, lower=True, unit_diagonal=True)
    u = jnp.einsum('bhnts,bhnsv->bhntv', tinv, vh * beta[..., None])
    w = jnp.einsum('bhnts,bhnsk->bhntk', tinv, kb * jnp.exp(gc)[..., None])
    qk = jnp.einsum('bhntk,bhnsk->bhnts', qh, kh) * decay
    qg = qh * jnp.exp(gc)[..., None]
    kg = kh * jnp.exp(gc[..., -1:] - gc)[..., None]
    g_last = jnp.exp(gc[..., -1])

    def step(state, inp):
        ui, wi, qgi, qki, kgi, gli = inp
        v_new = ui - jnp.einsum('bhck,bhkv->bhcv', wi, state)
        out = jnp.einsum('bhck,bhkv->bhcv', qgi, state) + jnp.einsum('bhts,bhsv->bhtv', qki, v_new)
        state = state * gli[:, :, None, None] + jnp.einsum('bhck,bhcv->bhkv', kgi, v_new)
        return state, out

    xs = (jnp.moveaxis(u, 2, 0), jnp.moveaxis(w, 2, 0), jnp.moveaxis(qg, 2, 0),
          jnp.moveaxis(qk, 2, 0), jnp.moveaxis(kg, 2, 0), jnp.moveaxis(g_last, 2, 0))
    s0 = jnp.zeros((B_, H_C, DK_C, DV_C), F32)
    _, oc = lax.scan(step, s0, xs)
    o = oc.transpose(1, 0, 3, 2, 4).reshape(B_, S_, H_C, DV_C)
    o = rms_norm(o, norm_g) * jax.nn.silu(z.astype(F32)).reshape(B_, S_, H_C, DV_C)
    return o.reshape(B_, S_, H_C * DV_C).astype(q.dtype)


def mixer_ab(x, positions, layer_idx, w_in, diff_lambda, diff_norm, gla_w2, gla_b2, gla_norm, w_out):
    h = x @ w_in
    aq, ak, av, bq, bk, bv, br, bg = _split(h, P0_SPLITS)
    oa = diff_attention(aq, ak, av, diff_lambda, diff_norm, positions, layer_idx)
    ob = gla(bq, bk, bv, bg, br, gla_w2, gla_b2, gla_norm)
    return jnp.concatenate([oa, ob], axis=-1) @ w_out


def mixer_c(x, w_in, conv_w, a_log, dt_bias, norm_g, w_out):
    h = x @ w_in
    q, k, v, z, bl, al = _split(h, P1_SPLITS)
    qkv = jax.nn.silu(causal_dwconv(jnp.concatenate([q, k, v], axis=-1), conv_w))
    q, k, v = _split(qkv, P1_SPLITS[:3])
    o = gated_deltanet(q, k, v, bl, al, z, a_log, dt_bias, norm_g)
    return o @ w_out


def memory_cross_attention(x, mem, wq, wkv, wo):
    B_, S_, _ = x.shape
    q = (x @ wq).reshape(B_, S_, N_XA, D_XA)
    k, v = _split(mem @ wkv, (D_MODEL, D_MODEL))
    k = k.reshape(B_, N_MEM, N_XA, D_XA)
    v = v.reshape(B_, N_MEM, N_XA, D_XA)
    s = jnp.einsum('bqhd,bkhd->bhqk', q, k).astype(F32) * (D_XA ** -0.5)
    p = jax.nn.softmax(s, axis=-1).astype(v.dtype)
    o = jnp.einsum('bhqk,bkhd->bqhd', p, v).reshape(B_, S_, D_MODEL)
    return o @ wo


def conv_ffn(x, w_in, conv_w, conv_b, w_out):
    h = causal_dwconv(x @ w_in, conv_w, conv_b)
    g, u = _split(h, (D_FF, D_FF))
    return (jax.nn.silu(g) * u) @ w_out


def setup_inputs(seed: int = 0) -> dict:
    key = jax.random.key(seed)
    keys = list(jax.random.split(key, 64))

    def nk():
        return keys.pop()

    def normal(shape, scale):
        return jax.random.normal(nk(), shape, F32) * scale

    def dense(fi, fo, scale=1.0):
        return normal((fi, fo), (fi ** -0.5) * scale)

    def gain(n):
        return 1.0 + normal((n,), 0.02)

    def bias(n):
        return normal((n,), 0.02)

    p = {}
    p['x'] = normal((BATCH, SEQ, D_MODEL), 1.0)
    p['mem'] = normal((BATCH, N_MEM, D_MODEL), 1.0)
    p['positions'] = jnp.broadcast_to(jnp.arange(SEQ, dtype=jnp.int32), (BATCH, SEQ))
    for l in range(DEPTH):
        if l % 2 == 0:
            p[f'w_in_{l}'] = dense(D_MODEL, P0)
            p[f'diff_lambda_{l}'] = normal((4, D_A), 0.1)
            p[f'diff_norm_{l}'] = gain(DV_A)
            p[f'gla_w2_{l}'] = dense(GATE_RANK, H_B * DK_B)
            p[f'gla_b2_{l}'] = bias(H_B * DK_B)
            p[f'gla_norm_{l}'] = gain(DV_B)
            p[f'w_mix_out_{l}'] = dense(H_A * DV_A + H_B * DV_B, D_MODEL, DEEPNORM_BETA)
        else:
            p[f'w_in_{l}'] = dense(D_MODEL, P1)
            p[f'gdn_conv_w_{l}'] = normal((CONV_C, 3 * H_C * DK_C), CONV_C ** -0.5)
            p[f'gdn_a_log_{l}'] = jnp.log(jax.random.uniform(nk(), (H_C,), F32, 1.0, 16.0))
            dt = jnp.exp(jax.random.uniform(nk(), (H_C,), F32, math.log(0.001), math.log(0.1)))
            p[f'gdn_dt_bias_{l}'] = dt + jnp.log(-jnp.expm1(-dt))
            p[f'gdn_norm_{l}'] = gain(DV_C)
            p[f'w_mix_out_{l}'] = dense(H_C * DV_C, D_MODEL, DEEPNORM_BETA)
        p[f'ln1_g_{l}'] = gain(D_MODEL)
        p[f'ln1_b_{l}'] = bias(D_MODEL)
        p[f'xa_wq_{l}'] = dense(D_MODEL, D_MODEL)
        p[f'xa_wkv_{l}'] = dense(D_MODEL, 2 * D_MODEL)
        p[f'xa_wo_{l}'] = dense(D_MODEL, D_MODEL, DEEPNORM_BETA)
        p[f'ln2_g_{l}'] = gain(D_MODEL)
        p[f'ln2_b_{l}'] = bias(D_MODEL)
        p[f'ffn_w_in_{l}'] = dense(D_MODEL, 2 * D_FF)
        p[f'ffn_conv_w_{l}'] = normal((FFN_CONV, 2 * D_FF), FFN_CONV ** -0.5)
        p[f'ffn_conv_b_{l}'] = bias(2 * D_FF)
        p[f'ffn_w_out_{l}'] = dense(D_FF, D_MODEL, DEEPNORM_BETA)
        p[f'ln3_g_{l}'] = gain(D_MODEL)
        p[f'ln3_b_{l}'] = bias(D_MODEL)
    return p


def reference(x, mem, positions,
              w_in_0, diff_lambda_0, diff_norm_0, gla_w2_0, gla_b2_0, gla_norm_0, w_mix_out_0,
              ln1_g_0, ln1_b_0, xa_wq_0, xa_wkv_0, xa_wo_0, ln2_g_0, ln2_b_0,
              ffn_w_in_0, ffn_conv_w_0, ffn_conv_b_0, ffn_w_out_0, ln3_g_0, ln3_b_0,
              w_in_1, gdn_conv_w_1, gdn_a_log_1, gdn_dt_bias_1, gdn_norm_1, w_mix_out_1,
              ln1_g_1, ln1_b_1, xa_wq_1, xa_wkv_1, xa_wo_1, ln2_g_1, ln2_b_1,
              ffn_w_in_1, ffn_conv_w_1, ffn_conv_b_1, ffn_w_out_1, ln3_g_1, ln3_b_1):
    mixer_params = [
        (w_in_0, diff_lambda_0, diff_norm_0, gla_w2_0, gla_b2_0, gla_norm_0, w_mix_out_0),
        (w_in_1, gdn_conv_w_1, gdn_a_log_1, gdn_dt_bias_1, gdn_norm_1, w_mix_out_1),
    ]
    common_params = [
        (ln1_g_0, ln1_b_0, xa_wq_0, xa_wkv_0, xa_wo_0, ln2_g_0, ln2_b_0,
         ffn_w_in_0, ffn_conv_w_0, ffn_conv_b_0, ffn_w_out_0, ln3_g_0, ln3_b_0),
        (ln1_g_1, ln1_b_1, xa_wq_1, xa_wkv_1, xa_wo_1, ln2_g_1, ln2_b_1,
         ffn_w_in_1, ffn_conv_w_1, ffn_conv_b_1, ffn_w_out_1, ln3_g_1, ln3_b_1),
    ]
    h = x
    for l in range(DEPTH):
        (ln1_g, ln1_b, xa_wq, xa_wkv, xa_wo, ln2_g, ln2_b,
         f_in, f_cw, f_cb, f_out, ln3_g, ln3_b) = common_params[l]
        if l % 2 == 0:
            mix = mixer_ab(h, positions, l, *mixer_params[l])
        else:
            mix = mixer_c(h, *mixer_params[l])
        h = layer_norm(DEEPNORM_ALPHA * h + mix, ln1_g, ln1_b)
        h = layer_norm(DEEPNORM_ALPHA * h + memory_cross_attention(h, mem, xa_wq, xa_wkv, xa_wo), ln2_g, ln2_b)
        h = layer_norm(DEEPNORM_ALPHA * h + conv_ffn(h, f_in, f_cw, f_cb, f_out), ln3_g, ln3_b)
    return h
```

```python
import functools
import math

import jax
import jax.numpy as jnp
from jax import lax
from jax.experimental import pallas as pl
from jax.experimental.pallas import tpu as pltpu

F32 = jnp.float32
BF16 = jnp.bfloat16

D_MODEL = 1024
N_MEM = 256
H_A, D_A, DV_A = 4, 64, 128
ROPE_THETA, ROPE_DIM = 500000.0, 16
H_B, DK_B, DV_B = 4, 64, 128
GATE_RANK, GATE_NORMALIZER = 16, 16.0
H_C, DK_C, DV_C, CONV_C = 8, 128, 128, 4
CHUNK = 64
N_XA, D_XA = 4, 256
D_FF = 2816
FFN_CONV = 3
LN_EPS, RMS_EPS = 1e-5, 1e-6
DEPTH = 2
ALPHA = (2.0 * DEPTH) ** 0.25
P0_MAIN = 3072
P1_MAIN = 4096
LANES = 128
FF_CHUNK = 256
N_FF_CHUNKS = D_FF // FF_CHUNK
HALO = 16
VMEM_LIMIT = 56 * 1024 * 1024

NT_DIMS = (((1,), (1,)), ((), ()))
TN_DIMS = (((0,), (0,)), ((), ()))
HI = lax.Precision.HIGHEST


def _cparams(sem):
    return pltpu.CompilerParams(dimension_semantics=sem, vmem_limit_bytes=VMEM_LIMIT)


def _const_spec(shape):
    nd = len(shape)
    return pl.BlockSpec(shape, lambda *_: (0,) * nd, pipeline_mode=pl.Buffered(1))


def _dot(a, b):
    return jnp.dot(a, b, preferred_element_type=F32)


def _dot_nt(a, b, precision=None):
    return lax.dot_general(a, b, NT_DIMS, precision=precision, preferred_element_type=F32)


def _layer_norm(y, g, b):
    mu = jnp.mean(y, axis=-1, keepdims=True)
    yc = y - mu
    var = jnp.mean(yc * yc, axis=-1, keepdims=True)
    return yc * lax.rsqrt(var + LN_EPS) * g + b


def _sigmoid(x):
    return 1.0 / (1.0 + jnp.exp(-x))


def _softplus(x):
    return jnp.maximum(x, 0.0) + jnp.log(1.0 + jnp.exp(-jnp.abs(x)))


def _tril_mask(n, strict=False):
    r = lax.broadcasted_iota(jnp.int32, (n, n), 0)
    c = lax.broadcasted_iota(jnp.int32, (n, n), 1)
    return (c < r) if strict else (c <= r)


def _proj0_kernel(x_ref, pos_ref, rc_ref, w_ref, wt_ref, o_ref, t_ref):
    xb = x_ref[...].astype(BF16)
    ang = pos_ref[...].astype(F32) * rc_ref[0:1, :]
    c = jnp.cos(ang)
    s = jnp.sin(ang)
    s_lo = s * rc_ref[1:2, :]
    s_hi = s * rc_ref[2:3, :]
    q_scale = D_A ** -0.5
    for j in range(2):
        y = _dot(xb, w_ref[:, j * 512:(j + 1) * 512])
        for t in range(4):
            yt = y[:, t * LANES:(t + 1) * LANES]
            yt = yt * c + pltpu.roll(yt, LANES - 8, 1) * s_lo + pltpu.roll(yt, 8, 1) * s_hi
            if j == 0:
                yt = yt * q_scale
            col = j * 512 + t * LANES
            o_ref[:, col:col + LANES] = yt.astype(BF16)
    for col in range(1024, P0_MAIN, 512):
        o_ref[:, col:col + 512] = _dot(xb, w_ref[:, col:col + 512]).astype(BF16)
    t_ref[...] = _dot(xb, wt_ref[...])


def _proj0(x2d, pos2d, rope_c, w_main, w_tail, *, tm):
    T = x2d.shape[0]
    return pl.pallas_call(
        _proj0_kernel,
        grid=(T // tm,),
        in_specs=[
            pl.BlockSpec((tm, D_MODEL), lambda i: (i, 0)),
            pl.BlockSpec((tm, 1), lambda i: (i, 0)),
            _const_spec((8, LANES)),
            _const_spec((D_MODEL, P0_MAIN)),
            _const_spec((D_MODEL, LANES)),
        ],
        out_specs=[
            pl.BlockSpec((tm, P0_MAIN), lambda i: (i, 0)),
            pl.BlockSpec((tm, LANES), lambda i: (i, 0)),
        ],
        out_shape=[
            jax.ShapeDtypeStruct((T, P0_MAIN), BF16),
            jax.ShapeDtypeStruct((T, LANES), F32),
        ],
        compiler_params=_cparams(("parallel",)),
        name="proj0",
    )(x2d, pos2d, rope_c, w_main, w_tail)


def _proj1_kernel(x_ref, w_ref, wt_ref, o_ref, t_ref):
    xb = x_ref[...]
    for col in range(0, P1_MAIN, 512):
        o_ref[:, col:col + 512] = _dot(xb, w_ref[:, col:col + 512]).astype(BF16)
    t_ref[...] = _dot(xb, wt_ref[...])


def _proj1(xb2d, w_main, w_tail, *, tm):
    T = xb2d.shape[0]
    return pl.pallas_call(
        _proj1_kernel,
        grid=(T // tm,),
        in_specs=[
            pl.BlockSpec((tm, D_MODEL), lambda i: (i, 0)),
            _const_spec((D_MODEL, P1_MAIN)),
            _const_spec((D_MODEL, LANES)),
        ],
        out_specs=[
            pl.BlockSpec((tm, P1_MAIN), lambda i: (i, 0)),
            pl.BlockSpec((tm, LANES), lambda i: (i, 0)),
        ],
        out_shape=[
            jax.ShapeDtypeStruct((T, P1_MAIN), BF16),
            jax.ShapeDtypeStruct((T, LANES), F32),
        ],
        compiler_params=_cparams(("parallel",)),
        name="proj1",
    )(xb2d, w_main, w_tail)


def _kvproj_kernel(m_ref, w_ref, o_ref):
    mb = m_ref[...].astype(BF16)
    for col in range(0, 2 * D_MODEL, 512):
        o_ref[:, col:col + 512] = _dot(mb, w_ref[:, col:col + 512]).astype(BF16)


def _kvproj(mem2d, wkv, *, tm):
    R = mem2d.shape[0]
    return pl.pallas_call(
        _kvproj_kernel,
        grid=(R // tm,),
        in_specs=[
            pl.BlockSpec((tm, D_MODEL), lambda i: (i, 0)),
            _const_spec((D_MODEL, 2 * D_MODEL)),
        ],
        out_specs=pl.BlockSpec((tm, 2 * D_MODEL), lambda i: (i, 0)),
        out_shape=jax.ShapeDtypeStruct((R, 2 * D_MODEL), BF16),
        compiler_params=_cparams(("parallel",)),
        name="kvproj",
    )(mem2d, wkv)


def _diffattn_kernel(q_ref, k_ref, v_ref, lam_ref, g_ref, o_ref, m_sc, l_sc, acc_sc, *, tq, lam_init):
    i = pl.program_id(2)
    q = q_ref[...]
    lane = lax.broadcasted_iota(jnp.int32, (tq, LANES), 1)
    zero = jnp.zeros_like(q)
    qq = jnp.concatenate([jnp.where(lane < D_A, q, zero), jnp.where(lane >= D_A, q, zero)], axis=0)
    m_sc[...] = jnp.full(m_sc.shape, -jnp.inf, F32)
    l_sc[...] = jnp.zeros(l_sc.shape, F32)
    acc_sc[...] = jnp.zeros(acc_sc.shape, F32)

    def step(j, masked):
        start = pl.multiple_of(j * tq, tq)
        kb = k_ref[pl.ds(start, tq), :]
        vb = v_ref[pl.ds(start, tq), :]
        s = _dot_nt(qq, kb)
        if masked:
            r = lax.broadcasted_iota(jnp.int32, (2 * tq, tq), 0)
            c = lax.broadcasted_iota(jnp.int32, (2 * tq, tq), 1)
            s = jnp.where(c <= jnp.where(r >= tq, r - tq, r), s, -jnp.inf)
        m_prev = m_sc[...]
        m_new = jnp.maximum(m_prev, jnp.max(s, axis=-1, keepdims=True))
        a = jnp.exp(m_prev - m_new)
        p = jnp.exp(s - m_new)
        l_sc[...] = a * l_sc[...] + jnp.sum(p, axis=-1, keepdims=True)
        acc_sc[...] = a * acc_sc[...] + _dot(p.astype(BF16), vb)
        m_sc[...] = m_new

    def body(j, carry):
        step(j, False)
        return carry

    lax.fori_loop(0, i, body, 0)
    step(i, True)

    o_all = acc_sc[...] / l_sc[...]
    lv = lam_ref[...]
    lam = (jnp.exp(jnp.sum(lv[0:1] * lv[1:2], axis=-1, keepdims=True))
           - jnp.exp(jnp.sum(lv[2:3] * lv[3:4], axis=-1, keepdims=True)) + lam_init)
    o = o_all[:tq] - lam * o_all[tq:]
    ms = jnp.mean(o * o, axis=-1, keepdims=True)
    y = o * lax.rsqrt(ms + RMS_EPS) * g_ref[...] * (1.0 - lam_init)
    o_ref[...] = y.astype(BF16)


def _diffattn(hp, lam_vecs, norm_g, *, B, S, tq, lam_init):
    nq = S // tq
    T = B * S
    kern = functools.partial(_diffattn_kernel, tq=tq, lam_init=lam_init)
    return pl.pallas_call(
        kern,
        grid=(B, H_A, nq),
        in_specs=[
            pl.BlockSpec((tq, LANES), lambda b, h, i: (b * nq + i, h)),
            pl.BlockSpec((S, LANES), lambda b, h, i: (b, H_A + h)),
            pl.BlockSpec((S, LANES), lambda b, h, i: (b, 2 * H_A + h)),
            pl.BlockSpec((4, D_A), lambda b, h, i: (0, 0)),
            pl.BlockSpec((1, DV_A), lambda b, h, i: (0, 0)),
        ],
        out_specs=pl.BlockSpec((tq, LANES), lambda b, h, i: (b * nq + i, h)),
        out_shape=jax.ShapeDtypeStruct((T, H_A * DV_A), BF16),
        scratch_shapes=[
            pltpu.VMEM((2 * tq, 1), F32),
            pltpu.VMEM((2 * tq, 1), F32),
            pltpu.VMEM((2 * tq, DV_A), F32),
        ],
        compiler_params=_cparams(("parallel", "parallel", "arbitrary")),
        name="diffattn",
    )(hp, hp, hp, lam_vecs, norm_g)


def _gla_kernel(q_ref, k_ref, v_ref, r_ref, t_ref, w2_ref, b2_ref, g_ref, o_ref, st_sc, *, rb):
    @pl.when(pl.program_id(1) == 0)
    def _():
        st_sc[...] = jnp.zeros(st_sc.shape, F32)

    tril = _tril_mask(CHUNK).astype(F32)
    incl = _tril_mask(CHUNK)
    for c in range(rb // CHUNK):
        rows = slice(c * CHUNK, (c + 1) * CHUNK)
        x = jnp.dot(t_ref[rows, :], w2_ref[...], precision=HI, preferred_element_type=F32) + b2_ref[...]
        gk = (jnp.minimum(x, 0.0) - jnp.log(1.0 + jnp.exp(-jnp.abs(x)))) * (1.0 / GATE_NORMALIZER)
        b = jnp.dot(tril, gk, precision=HI, preferred_element_type=F32)
        q = q_ref[rows, :].astype(F32) * (DK_B ** -0.5)
        k = k_ref[rows, :].astype(F32)
        v = v_ref[rows, :]
        qe = (q * jnp.exp(b)).astype(BF16)
        ke = (k * jnp.exp(-b)).astype(BF16)
        bl = b[CHUNK - 1:CHUNK, :]
        kl = (k * jnp.exp(bl - b)).astype(BF16)
        ebl = jnp.exp(bl)
        vt = v.astype(F32).T.astype(BF16)
        for h in range(H_B):
            ks = slice(h * DK_B, (h + 1) * DK_B)
            vs = slice(h * DV_B, (h + 1) * DV_B)
            att = jnp.where(incl, _dot_nt(qe[:, ks], ke[:, ks]), 0.0)
            st = st_sc[h]
            o = _dot_nt(qe[:, ks], st.astype(BF16)) + _dot(att.astype(BF16), v[:, vs])
            st_sc[h] = st * ebl[:, ks] + _dot(vt[vs, :], kl[:, ks])
            ms = jnp.mean(o * o, axis=-1, keepdims=True)
            r = r_ref[rows, vs].astype(F32)
            y = o * lax.rsqrt(ms + RMS_EPS) * g_ref[...] * (r * _sigmoid(r))
            o_ref[rows, vs] = y.astype(BF16)


def _gla(hp, tail, w2p, b2, norm_g, *, B, S, rb):
    nb = S // rb
    T = B * S
    kern = functools.partial(_gla_kernel, rb=rb)
    return pl.pallas_call(
        kern,
        grid=(B, nb),
        in_specs=[
            pl.BlockSpec((rb, H_B * DK_B), lambda b, i: (b * nb + i, 6)),
            pl.BlockSpec((rb, H_B * DK_B), lambda b, i: (b * nb + i, 7)),
            pl.BlockSpec((rb, H_B * DV_B), lambda b, i: (b * nb + i, 4)),
            pl.BlockSpec((rb, H_B * DV_B), lambda b, i: (b * nb + i, 5)),
            pl.BlockSpec((rb, LANES), lambda b, i: (b * nb + i, 0)),
            pl.BlockSpec((LANES, H_B * DK_B), lambda b, i: (0, 0)),
            pl.BlockSpec((1, H_B * DK_B), lambda b, i: (0, 0)),
            pl.BlockSpec((1, DV_B), lambda b, i: (0, 0)),
        ],
        out_specs=pl.BlockSpec((rb, H_B * DV_B), lambda b, i: (b * nb + i, 0)),
        out_shape=jax.ShapeDtypeStruct((T, H_B * DV_B), BF16),
        scratch_shapes=[pltpu.VMEM((H_B, DV_B, DK_B), F32)],
        compiler_params=_cparams(("parallel", "arbitrary")),
        name="gla",
    )(hp, hp, hp, hp, tail, w2p, b2, norm_g)


def _gdn_kernel(q_ref, k_ref, v_ref, z_ref, t_ref, wq_ref, wk_ref, wv_ref, hp_ref, g_ref, o_ref,
                s_sc, carry_sc, xext_sc, *, rb):
    h = pl.program_id(1)

    @pl.when(pl.program_id(2) == 0)
    def _():
        s_sc[...] = jnp.zeros(s_sc.shape, F32)
        carry_sc[...] = jnp.zeros(carry_sc.shape, F32)

    def conv_silu(x_ref, w_ref, idx):
        xext_sc[idx, 0:8, :] = carry_sc[idx]
        xext_sc[idx, 8:8 + rb, :] = x_ref[...].astype(F32)
        w = w_ref[...]
        y = w[3:4, :] * xext_sc[idx, 8:8 + rb, :]
        for j in range(CONV_C - 1):
            y = y + w[j:j + 1, :] * xext_sc[idx, 5 + j:5 + j + rb, :]
        carry_sc[idx] = xext_sc[idx, rb:rb + 8, :]
        return y * _sigmoid(y)

    q = conv_silu(q_ref, wq_ref, 0)
    k = conv_silu(k_ref, wk_ref, 1)
    v = conv_silu(v_ref, wv_ref, 2)
    q = q * lax.rsqrt(jnp.sum(q * q, axis=-1, keepdims=True) + RMS_EPS) * (DK_C ** -0.5)
    k = k * lax.rsqrt(jnp.sum(k * k, axis=-1, keepdims=True) + RMS_EPS)

    t = t_ref[...]
    lane = lax.broadcasted_iota(jnp.int32, (rb, LANES), 1)
    beta_all = _sigmoid(t)
    g_all = -jnp.exp(hp_ref[0:1, :]) * _softplus(t + hp_ref[1:2, :])
    beta = jnp.sum(jnp.where(lane == h, beta_all, 0.0), axis=-1, keepdims=True)
    g = jnp.sum(jnp.where(lane == h + H_C, g_all, 0.0), axis=-1, keepdims=True)

    tril = _tril_mask(CHUNK).astype(F32)
    incl = _tril_mask(CHUNK)
    strict = _tril_mask(CHUNK, strict=True)
    eye = (lax.broadcasted_iota(jnp.int32, (CHUNK, CHUNK), 0)
           == lax.broadcasted_iota(jnp.int32, (CHUNK, CHUNK), 1)).astype(F32)
    lane_c = lax.broadcasted_iota(jnp.int32, (CHUNK, LANES), 1)

    for c in range(rb // CHUNK):
        rows = slice(c * CHUNK, (c + 1) * CHUNK)
        qc, kc, vc = q[rows], k[rows], v[rows]
        bc = beta[rows]
        gb = jnp.broadcast_to(g[rows], (CHUNK, LANES))
        cum = jnp.dot(tril, gb, precision=HI, preferred_element_type=F32)
        a_m = jnp.where(lane_c == 0, cum, jnp.where(lane_c == 1, 1.0, 0.0))
        b_m = jnp.where(lane_c == 0, 1.0, jnp.where(lane_c == 1, -cum, 0.0))
        diff = _dot_nt(a_m, b_m, precision=HI)
        dec = jnp.where(incl, jnp.exp(diff), 0.0)
        eg = jnp.exp(cum)
        kb = kc * bc
        kbb = kb.astype(BF16)
        kcb = kc.astype(BF16)
        m = jnp.where(strict, _dot_nt(kbb, kcb) * dec, 0.0)
        p = -m
        tinv = eye + p
        for _ in range(5):
            pb = p.astype(BF16)
            p = _dot(pb, pb)
            tinv = tinv + _dot(p.astype(BF16), tinv.astype(BF16))
        rhs = jnp.concatenate([vc * bc, kb * eg], axis=1).astype(BF16)
        uw = _dot(tinv.astype(BF16), rhs)
        u, w = uw[:, :DV_C], uw[:, DV_C:]
        qk = _dot_nt(qc.astype(BF16), kcb) * dec
        st = s_sc[...]
        ws = _dot(jnp.concatenate([w, qc * eg], axis=0).astype(BF16), st.astype(BF16))
        v_new = u - ws[:CHUNK]
        o = ws[CHUNK:] + _dot(qk.astype(BF16), v_new.astype(BF16))
        gl = cum[CHUNK - 1:CHUNK, :]
        kg = kc * jnp.exp(gl - cum)
        s_sc[...] = st * jnp.exp(gl) + _dot(kg.T.astype(BF16), v_new.astype(BF16))
        ms = jnp.mean(o * o, axis=-1, keepdims=True)
        z = z_ref[rows, :].astype(F32)
        y = o * lax.rsqrt(ms + RMS_EPS) * g_ref[...] * (z * _sigmoid(z))
        o_ref[rows, :] = y.astype(BF16)


def _gdn(hp, tail, conv_w, head_par, norm_g, *, B, S, rb):
    nb = S // rb
    T = B * S
    kern = functools.partial(_gdn_kernel, rb=rb)
    row = lambda b, h, i: b * nb + i
    return pl.pallas_call(
        kern,
        grid=(B, H_C, nb),
        in_specs=[
            pl.BlockSpec((rb, LANES), lambda b, h, i: (row(b, h, i), h)),
            pl.BlockSpec((rb, LANES), lambda b, h, i: (row(b, h, i), H_C + h)),
            pl.BlockSpec((rb, LANES), lambda b, h, i: (row(b, h, i), 2 * H_C + h)),
            pl.BlockSpec((rb, LANES), lambda b, h, i: (row(b, h, i), 3 * H_C + h)),
            pl.BlockSpec((rb, LANES), lambda b, h, i: (row(b, h, i), 0)),
            pl.BlockSpec((CONV_C, LANES), lambda b, h, i: (0, h)),
            pl.BlockSpec((CONV_C, LANES), lambda b, h, i: (0, H_C + h)),
            pl.BlockSpec((CONV_C, LANES), lambda b, h, i: (0, 2 * H_C + h)),
            pl.BlockSpec((8, LANES), lambda b, h, i: (0, 0)),
            pl.BlockSpec((1, DV_C), lambda b, h, i: (0, 0)),
        ],
        out_specs=pl.BlockSpec((rb, LANES), lambda b, h, i: (row(b, h, i), h)),
        out_shape=jax.ShapeDtypeStruct((T, H_C * DV_C), BF16),
        scratch_shapes=[
            pltpu.VMEM((DK_C, DV_C), F32),
            pltpu.VMEM((3, 8, LANES), F32),
            pltpu.VMEM((3, rb + 8, LANES), F32),
        ],
        compiler_params=_cparams(("parallel", "parallel", "arbitrary")),
        name="gdn",
    )(hp, hp, hp, hp, tail, conv_w, conv_w, conv_w, head_par, norm_g)


def _outproj_ln_kernel(*refs, n_in):
    ins = refs[:n_in]
    w_ref, res_ref, g_ref, b_ref, o_ref, ob_ref = refs[n_in:]
    y = ALPHA * res_ref[...]
    off = 0
    for r in ins:
        kdim = r.shape[1]
        y = y + _dot(r[...], w_ref[off:off + kdim, :])
        off += kdim
    out = _layer_norm(y, g_ref[...], b_ref[...])
    o_ref[...] = out
    ob_ref[...] = out.astype(BF16)


def _outproj_ln(ins, w, res, g, b, *, tm):
    T = res.shape[0]
    kern = functools.partial(_outproj_ln_kernel, n_in=len(ins))
    in_specs = [pl.BlockSpec((tm, a.shape[1]), lambda i: (i, 0)) for a in ins]
    in_specs += [
        _const_spec(w.shape),
        pl.BlockSpec((tm, D_MODEL), lambda i: (i, 0)),
        pl.BlockSpec((1, D_MODEL), lambda i: (0, 0)),
        pl.BlockSpec((1, D_MODEL), lambda i: (0, 0)),
    ]
    return pl.pallas_call(
        kern,
        grid=(T // tm,),
        in_specs=in_specs,
        out_specs=[pl.BlockSpec((tm, D_MODEL), lambda i: (i, 0))] * 2,
        out_shape=[jax.ShapeDtypeStruct((T, D_MODEL), F32), jax.ShapeDtypeStruct((T, D_MODEL), BF16)],
        compiler_params=_cparams(("parallel",)),
        name="outproj_ln",
    )(*ins, w, res, g, b)


def _xattn_kernel(hb_ref, h_ref, kv_ref, wq_ref, wo_ref, g_ref, b_ref, o_ref, ob_ref, att_sc):
    q = (_dot(hb_ref[...], wq_ref[...]) * (D_XA ** -0.5)).astype(BF16)
    for hd in range(N_XA):
        cs = slice(hd * D_XA, (hd + 1) * D_XA)
        s = _dot_nt(q[:, cs], kv_ref[:, cs])
        p = jnp.exp(s - jnp.max(s, axis=-1, keepdims=True))
        l = jnp.sum(p, axis=-1, keepdims=True)
        vs = slice(D_MODEL + hd * D_XA, D_MODEL + (hd + 1) * D_XA)
        att_sc[:, cs] = (_dot(p.astype(BF16), kv_ref[:, vs]) / l).astype(BF16)
    y = ALPHA * h_ref[...] + _dot(att_sc[...], wo_ref[...])
    out = _layer_norm(y, g_ref[...], b_ref[...])
    o_ref[...] = out
    ob_ref[...] = out.astype(BF16)


def _xattn(hb, h, kv, wq, wo, g, b, *, S, tm):
    T = h.shape[0]
    tiles_per_seq = S // tm
    return pl.pallas_call(
        _xattn_kernel,
        grid=(T // tm,),
        in_specs=[
            pl.BlockSpec((tm, D_MODEL), lambda i: (i, 0)),
            pl.BlockSpec((tm, D_MODEL), lambda i: (i, 0)),
            pl.BlockSpec((N_MEM, 2 * D_MODEL), lambda i: (i // tiles_per_seq, 0)),
            _const_spec((D_MODEL, D_MODEL)),
            _const_spec((D_MODEL, D_MODEL)),
            pl.BlockSpec((1, D_MODEL), lambda i: (0, 0)),
            pl.BlockSpec((1, D_MODEL), lambda i: (0, 0)),
        ],
        out_specs=[pl.BlockSpec((tm, D_MODEL), lambda i: (i, 0))] * 2,
        out_shape=[jax.ShapeDtypeStruct((T, D_MODEL), F32), jax.ShapeDtypeStruct((T, D_MODEL), BF16)],
        scratch_shapes=[pltpu.VMEM((tm, D_MODEL), BF16)],
        compiler_params=_cparams(("parallel",)),
        name="xattn",
    )(hb, h, kv, wq, wo, g, b)


def _ffn_kernel(hprev_ref, hb_ref, h_ref, win_ref, cw_ref, cb_ref, wout_ref, g_ref, b_ref, o_ref, ob_ref,
                xcat_sc, hid_sc, act_sc, *, tm, tiles_per_seq):
    is_start = (pl.program_id(0) % tiles_per_seq) == 0
    prev = hprev_ref[...]
    xcat_sc[0:HALO, :] = jnp.where(is_start, jnp.zeros_like(prev), prev)
    xcat_sc[HALO:HALO + tm, :] = hb_ref[...]
    xc = xcat_sc[...]

    def conv(c):
        hid_sc[...] = _dot(xc, win_ref[c])
        w = cw_ref[c]
        y = cb_ref[c] + w[FFN_CONV - 1:FFN_CONV, :] * hid_sc[HALO:HALO + tm, :]
        for j in range(FFN_CONV - 1):
            off = HALO - (FFN_CONV - 1) + j
            y = y + w[j:j + 1, :] * hid_sc[off:off + tm, :]
        return y

    for c in range(N_FF_CHUNKS):
        gate = conv(c)
        up = conv(N_FF_CHUNKS + c)
        act_sc[:, c * FF_CHUNK:(c + 1) * FF_CHUNK] = (gate * _sigmoid(gate) * up).astype(BF16)
    y = ALPHA * h_ref[...] + _dot(act_sc[...], wout_ref[...])
    out = _layer_norm(y, g_ref[...], b_ref[...])
    o_ref[...] = out
    ob_ref[...] = out.astype(BF16)


def _ffn(hb, h, win_r, cw_r, cb_r, wout, g, b, *, S, tm):
    T = h.shape[0]
    tiles_per_seq = S // tm
    kern = functools.partial(_ffn_kernel, tm=tm, tiles_per_seq=tiles_per_seq)
    halo_blocks = tm // HALO
    return pl.pallas_call(
        kern,
        grid=(T // tm,),
        in_specs=[
            pl.BlockSpec((HALO, D_MODEL), lambda i: (jnp.maximum(i * halo_blocks - 1, 0), 0)),
            pl.BlockSpec((tm, D_MODEL), lambda i: (i, 0)),
            pl.BlockSpec((tm, D_MODEL), lambda i: (i, 0)),
            _const_spec(win_r.shape),
            _const_spec(cw_r.shape),
            _const_spec(cb_r.shape),
            _const_spec(wout.shape),
            pl.BlockSpec((1, D_MODEL), lambda i: (0, 0)),
            pl.BlockSpec((1, D_MODEL), lambda i: (0, 0)),
        ],
        out_specs=[pl.BlockSpec((tm, D_MODEL), lambda i: (i, 0))] * 2,
        out_shape=[jax.ShapeDtypeStruct((T, D_MODEL), F32), jax.ShapeDtypeStruct((T, D_MODEL), BF16)],
        scratch_shapes=[
            pltpu.VMEM((tm + HALO, D_MODEL), BF16),
            pltpu.VMEM((tm + HALO, FF_CHUNK), F32),
            pltpu.VMEM((tm, D_FF), BF16),
        ],
        compiler_params=_cparams(("parallel",)),
        name="ffn",
    )(hb, hb, h, win_r, cw_r, cb_r, wout, g, b)


def _rope_consts():
    inv_freq = ROPE_THETA ** (-jnp.arange(0, ROPE_DIM, 2, dtype=F32) / ROPE_DIM)
    d = jnp.arange(LANES) % D_A
    half = ROPE_DIM // 2
    freq = jnp.where(d < ROPE_DIM, inv_freq[d % half], 0.0)
    lo = jnp.where(d < half, -1.0, 0.0)
    hi = jnp.where((d >= half) & (d < ROPE_DIM), 1.0, 0.0)
    return jnp.zeros((8, LANES), F32).at[0].set(freq).at[1].set(lo).at[2].set(hi)


def _row(v):
    return v.reshape(1, -1).astype(F32)


def _tail_w(w, n_main):
    tail = w[:, n_main:]
    return jnp.pad(tail, ((0, 0), (0, LANES - tail.shape[1]))).astype(BF16)


def _ffn_params(f_in, f_cw, f_cb, f_out):
    n2 = 2 * N_FF_CHUNKS
    win_r = f_in.astype(BF16).reshape(D_MODEL, n2, FF_CHUNK).transpose(1, 0, 2)
    cw_r = f_cw.reshape(FFN_CONV, n2, FF_CHUNK).transpose(1, 0, 2)
    cb_r = f_cb.reshape(n2, 1, FF_CHUNK)
    return win_r, cw_r, cb_r, f_out.astype(BF16)


def _common(h, hb, mem2d, S, params, tm):
    (ln2_g, ln2_b, xa_wq, xa_wkv, xa_wo, f_in, f_cw, f_cb, f_out, ln3_g, ln3_b) = params
    kv = _kvproj(mem2d, xa_wkv.astype(BF16), tm=tm)
    h, hb = _xattn(hb, h, kv, xa_wq.astype(BF16), xa_wo.astype(BF16), _row(ln2_g), _row(ln2_b), S=S, tm=tm)
    win_r, cw_r, cb_r, wout = _ffn_params(f_in, f_cw, f_cb, f_out)
    return _ffn(hb, h, win_r, cw_r, cb_r, wout, _row(ln3_g), _row(ln3_b), S=S, tm=tm)


def kernel(x, mem, positions, w_in_0, diff_lambda_0, diff_norm_0, gla_w2_0, gla_b2_0, gla_norm_0, w_mix_out_0, ln1_g_0, ln1_b_0, xa_wq_0, xa_wkv_0, xa_wo_0, ln2_g_0, ln2_b_0, ffn_w_in_0, ffn_conv_w_0, ffn_conv_b_0, ffn_w_out_0, ln3_g_0, ln3_b_0, w_in_1, gdn_conv_w_1, gdn_a_log_1, gdn_dt_bias_1, gdn_norm_1, w_mix_out_1, ln1_g_1, ln1_b_1, xa_wq_1, xa_wkv_1, xa_wo_1, ln2_g_1, ln2_b_1, ffn_w_in_1, ffn_conv_w_1, ffn_conv_b_1, ffn_w_out_1, ln3_g_1, ln3_b_1):
    B, S, _ = x.shape
    T = B * S
    tm = min(512, S)
    x2d = x.reshape(T, D_MODEL)
    mem2d = mem.reshape(B * N_MEM, D_MODEL)

    hp, tail = _proj0(x2d, positions.reshape(T, 1), _rope_consts(),
                      w_in_0[:, :P0_MAIN].astype(BF16), _tail_w(w_in_0, P0_MAIN), tm=tm)
    lam_init = 0.8 - 0.6 * math.exp(-0.3 * 0)
    oa = _diffattn(hp, diff_lambda_0.astype(F32), _row(diff_norm_0), B=B, S=S, tq=min(256, S), lam_init=lam_init)
    w2p = jnp.pad(gla_w2_0.astype(F32), ((0, LANES - GATE_RANK), (0, 0)))
    ob = _gla(hp, tail, w2p, _row(gla_b2_0), _row(gla_norm_0), B=B, S=S, rb=min(256, S))
    h, hb = _outproj_ln([oa, ob], w_mix_out_0.astype(BF16), x2d, _row(ln1_g_0), _row(ln1_b_0), tm=tm)
    h, hb = _common(h, hb, mem2d, S,
                    (ln2_g_0, ln2_b_0, xa_wq_0, xa_wkv_0, xa_wo_0, ffn_w_in_0, ffn_conv_w_0, ffn_conv_b_0,
                     ffn_w_out_0, ln3_g_0, ln3_b_0), tm)

    hp, tail = _proj1(hb, w_in_1[:, :P1_MAIN].astype(BF16), _tail_w(w_in_1, P1_MAIN), tm=tm)
    head_par = (jnp.zeros((8, LANES), F32)
                .at[0, H_C:2 * H_C].set(gdn_a_log_1.astype(F32))
                .at[1, H_C:2 * H_C].set(gdn_dt_bias_1.astype(F32)))
    oc = _gdn(hp, tail, gdn_conv_w_1.astype(F32), head_par, _row(gdn_norm_1), B=B, S=S, rb=min(512, S))
    h, hb = _outproj_ln([oc], w_mix_out_1.astype(BF16), h, _row(ln1_g_1), _row(ln1_b_1), tm=tm)
    h, hb = _common(h, hb, mem2d, S,
                    (ln2_g_1, ln2_b_1, xa_wq_1, xa_wkv_1, xa_wo_1, ffn_w_in_1, ffn_conv_w_1, ffn_conv_b_1,
                     ffn_w_out_1, ln3_g_1, ln3_b_1), tm)
    return h.reshape(B, S, D_MODEL)
```

```python
import functools
import math

import jax
import jax.numpy as jnp
from jax import lax
from jax.experimental import pallas as pl
from jax.experimental.pallas import tpu as pltpu

F32 = jnp.float32
BF16 = jnp.bfloat16

D_MODEL = 1024
N_MEM = 256
H_A, D_A, DV_A = 4, 64, 128
ROPE_THETA, ROPE_DIM = 500000.0, 16
H_B, DK_B, DV_B = 4, 64, 128
GATE_RANK, GATE_NORMALIZER = 16, 16.0
H_C, DK_C, DV_C, CONV_C = 8, 128, 128, 4
CHUNK = 64
N_XA, D_XA = 4, 256
D_FF = 2816
FFN_CONV = 3
LN_EPS, RMS_EPS = 1e-5, 1e-6
DEPTH = 2
ALPHA = (2.0 * DEPTH) ** 0.25
P0_MAIN = 3072
P1_MAIN = 4096
LANES = 128
FF_CHUNK = 256
N_FF_CHUNKS = D_FF // FF_CHUNK
HALO = 16
VMEM_LIMIT = 56 * 1024 * 1024

NT_DIMS = (((1,), (1,)), ((), ()))
TN_DIMS = (((0,), (0,)), ((), ()))
HI = lax.Precision.HIGHEST


def _cparams(sem):
    return pltpu.CompilerParams(dimension_semantics=sem, vmem_limit_bytes=VMEM_LIMIT)


def _const_spec(shape):
    nd = len(shape)
    return pl.BlockSpec(shape, lambda *_: (0,) * nd, pipeline_mode=pl.Buffered(1))


def _dot(a, b):
    return jnp.dot(a, b, preferred_element_type=F32)


def _dot_nt(a, b, precision=None):
    return lax.dot_general(a, b, NT_DIMS, precision=precision, preferred_element_type=F32)


def _layer_norm(y, g, b):
    mu = jnp.mean(y, axis=-1, keepdims=True)
    yc = y - mu
    var = jnp.mean(yc * yc, axis=-1, keepdims=True)
    return yc * lax.rsqrt(var + LN_EPS) * g + b


def _sigmoid(x):
    return 1.0 / (1.0 + jnp.exp(-x))


def _softplus(x):
    return jnp.maximum(x, 0.0) + jnp.log(1.0 + jnp.exp(-jnp.abs(x)))


def _tril_mask(n, strict=False):
    r = lax.broadcasted_iota(jnp.int32, (n, n), 0)
    c = lax.broadcasted_iota(jnp.int32, (n, n), 1)
    return (c < r) if strict else (c <= r)


def _proj0_kernel(x_ref, pos_ref, rc_ref, w_ref, wt_ref, o_ref, t_ref):
    xb = x_ref[...].astype(BF16)
    ang = pos_ref[...].astype(F32) * rc_ref[0:1, :]
    c = jnp.cos(ang)
    s = jnp.sin(ang)
    s_lo = s * rc_ref[1:2, :]
    s_hi = s * rc_ref[2:3, :]
    q_scale = D_A ** -0.5
    for j in range(2):
        y = _dot(xb, w_ref[:, j * 512:(j + 1) * 512])
        for t in range(4):
            yt = y[:, t * LANES:(t + 1) * LANES]
            yt = yt * c + pltpu.roll(yt, LANES - 8, 1) * s_lo + pltpu.roll(yt, 8, 1) * s_hi
            if j == 0:
                yt = yt * q_scale
            col = j * 512 + t * LANES
            o_ref[:, col:col + LANES] = yt.astype(BF16)
    for col in range(1024, P0_MAIN, 512):
        o_ref[:, col:col + 512] = _dot(xb, w_ref[:, col:col + 512]).astype(BF16)
    t_ref[...] = _dot(xb, wt_ref[...])


def _proj0(x2d, pos2d, rope_c, w_main, w_tail, *, tm):
    T = x2d.shape[0]
    return pl.pallas_call(
        _proj0_kernel,
        grid=(T // tm,),
        in_specs=[
            pl.BlockSpec((tm, D_MODEL), lambda i: (i, 0)),
            pl.BlockSpec((tm, 1), lambda i: (i, 0)),
            _const_spec((8, LANES)),
            _const_spec((D_MODEL, P0_MAIN)),
            _const_spec((D_MODEL, LANES)),
        ],
        out_specs=[
            pl.BlockSpec((tm, P0_MAIN), lambda i: (i, 0)),
            pl.BlockSpec((tm, LANES), lambda i: (i, 0)),
        ],
        out_shape=[
            jax.ShapeDtypeStruct((T, P0_MAIN), BF16),
            jax.ShapeDtypeStruct((T, LANES), F32),
        ],
        compiler_params=_cparams(("parallel",)),
        name="proj0",
    )(x2d, pos2d, rope_c, w_main, w_tail)


def _proj1_kernel(x_ref, w_ref, wt_ref, o_ref, t_ref):
    xb = x_ref[...]
    for col in range(0, P1_MAIN, 512):
        o_ref[:, col:col + 512] = _dot(xb, w_ref[:, col:col + 512]).astype(BF16)
    t_ref[...] = _dot(xb, wt_ref[...])


def _proj1(xb2d, w_main, w_tail, *, tm):
    T = xb2d.shape[0]
    return pl.pallas_call(
        _proj1_kernel,
        grid=(T // tm,),
        in_specs=[
            pl.BlockSpec((tm, D_MODEL), lambda i: (i, 0)),
            _const_spec((D_MODEL, P1_MAIN)),
            _const_spec((D_MODEL, LANES)),
        ],
        out_specs=[
            pl.BlockSpec((tm, P1_MAIN), lambda i: (i, 0)),
            pl.BlockSpec((tm, LANES), lambda i: (i, 0)),
        ],
        out_shape=[
            jax.ShapeDtypeStruct((T, P1_MAIN), BF16),
            jax.ShapeDtypeStruct((T, LANES), F32),
        ],
        compiler_params=_cparams(("parallel",)),
        name="proj1",
    )(xb2d, w_main, w_tail)


def _kvproj_kernel(m_ref, w_ref, o_ref):
    mb = m_ref[...].astype(BF16)
    for col in range(0, 2 * D_MODEL, 512):
        o_ref[:, col:col + 512] = _dot(mb, w_ref[:, col:col + 512]).astype(BF16)


def _kvproj(mem2d, wkv, *, tm):
    R = mem2d.shape[0]
    return pl.pallas_call(
        _kvproj_kernel,
        grid=(R // tm,),
        in_specs=[
            pl.BlockSpec((tm, D_MODEL), lambda i: (i, 0)),
            _const_spec((D_MODEL, 2 * D_MODEL)),
        ],
        out_specs=pl.BlockSpec((tm, 2 * D_MODEL), lambda i: (i, 0)),
        out_shape=jax.ShapeDtypeStruct((R, 2 * D_MODEL), BF16),
        compiler_params=_cparams(("parallel",)),
        name="kvproj",
    )(mem2d, wkv)


def _diffattn_kernel(q_ref, k_ref, v_ref, lam_ref, g_ref, o_ref, *, S, tq, lam_init):
    lv = lam_ref[...]
    lam = (jnp.exp(jnp.sum(lv[0:1] * lv[1:2], axis=-1, keepdims=True))
           - jnp.exp(jnp.sum(lv[2:3] * lv[3:4], axis=-1, keepdims=True)) + lam_init)
    lane = lax.broadcasted_iota(jnp.int32, (tq, LANES), 1)
    r = lax.broadcasted_iota(jnp.int32, (2 * tq, tq), 0)
    c = lax.broadcasted_iota(jnp.int32, (2 * tq, tq), 1)
    causal = c <= jnp.where(r >= tq, r - tq, r)
    for qi in range(S // tq):
        n_keys = (qi + 1) * tq
        q = q_ref[qi * tq:(qi + 1) * tq, :]
        zero = jnp.zeros_like(q)
        qq = jnp.concatenate([jnp.where(lane < D_A, q, zero), jnp.where(lane >= D_A, q, zero)], axis=0)
        s = _dot_nt(qq, k_ref[0:n_keys, :])
        s_diag = jnp.where(causal, s[:, n_keys - tq:], -jnp.inf)
        s = s_diag if qi == 0 else jnp.concatenate([s[:, :n_keys - tq], s_diag], axis=1)
        p = jnp.exp(s - jnp.max(s, axis=-1, keepdims=True))
        l = jnp.sum(p, axis=-1, keepdims=True)
        o_all = _dot(p.astype(BF16), v_ref[0:n_keys, :]) / l
        o = o_all[:tq] - lam * o_all[tq:]
        ms = jnp.mean(o * o, axis=-1, keepdims=True)
        y = o * lax.rsqrt(ms + RMS_EPS) * g_ref[...] * (1.0 - lam_init)
        o_ref[qi * tq:(qi + 1) * tq, :] = y.astype(BF16)


def _diffattn(hp, lam_vecs, norm_g, *, B, S, tq, lam_init):
    T = B * S
    kern = functools.partial(_diffattn_kernel, S=S, tq=tq, lam_init=lam_init)
    return pl.pallas_call(
        kern,
        grid=(B, H_A),
        in_specs=[
            pl.BlockSpec((S, LANES), lambda b, h: (b, h)),
            pl.BlockSpec((S, LANES), lambda b, h: (b, H_A + h)),
            pl.BlockSpec((S, LANES), lambda b, h: (b, 2 * H_A + h)),
            pl.BlockSpec((4, D_A), lambda b, h: (0, 0)),
            pl.BlockSpec((1, DV_A), lambda b, h: (0, 0)),
        ],
        out_specs=pl.BlockSpec((S, LANES), lambda b, h: (b, h)),
        out_shape=jax.ShapeDtypeStruct((T, H_A * DV_A), BF16),
        compiler_params=_cparams(("parallel", "parallel")),
        name="diffattn",
    )(hp, hp, hp, lam_vecs, norm_g)


def _gla_kernel(q_ref, k_ref, v_ref, r_ref, t_ref, w2_ref, b2_ref, g_ref, o_ref, st_sc, *, rb):
    @pl.when(pl.program_id(1) == 0)
    def _():
        st_sc[...] = jnp.zeros(st_sc.shape, F32)

    tril = _tril_mask(CHUNK).astype(F32)
    incl = _tril_mask(CHUNK)
    for c in range(rb // CHUNK):
        rows = slice(c * CHUNK, (c + 1) * CHUNK)
        x = jnp.dot(t_ref[rows, :], w2_ref[...], precision=HI, preferred_element_type=F32) + b2_ref[...]
        gk = (jnp.minimum(x, 0.0) - jnp.log(1.0 + jnp.exp(-jnp.abs(x)))) * (1.0 / GATE_NORMALIZER)
        b = jnp.dot(tril, gk, precision=HI, preferred_element_type=F32)
        q = q_ref[rows, :].astype(F32) * (DK_B ** -0.5)
        k = k_ref[rows, :].astype(F32)
        v = v_ref[rows, :]
        qe = (q * jnp.exp(b)).astype(BF16)
        ke = (k * jnp.exp(-b)).astype(BF16)
        bl = b[CHUNK - 1:CHUNK, :]
        kl = (k * jnp.exp(bl - b)).astype(BF16)
        ebl = jnp.exp(bl)
        vt = v.astype(F32).T.astype(BF16)
        for h in range(H_B):
            ks = slice(h * DK_B, (h + 1) * DK_B)
            vs = slice(h * DV_B, (h + 1) * DV_B)
            att = jnp.where(incl, _dot_nt(qe[:, ks], ke[:, ks]), 0.0)
            st = st_sc[h]
            o = _dot_nt(qe[:, ks], st.astype(BF16)) + _dot(att.astype(BF16), v[:, vs])
            st_sc[h] = st * ebl[:, ks] + _dot(vt[vs, :], kl[:, ks])
            ms = jnp.mean(o * o, axis=-1, keepdims=True)
            r = r_ref[rows, vs].astype(F32)
            y = o * lax.rsqrt(ms + RMS_EPS) * g_ref[...] * (r * _sigmoid(r))
            o_ref[rows, vs] = y.astype(BF16)


def _gla(hp, tail, w2p, b2, norm_g, *, B, S, rb):
    nb = S // rb
    T = B * S
    kern = functools.partial(_gla_kernel, rb=rb)
    return pl.pallas_call(
        kern,
        grid=(B, nb),
        in_specs=[
            pl.BlockSpec((rb, H_B * DK_B), lambda b, i: (b * nb + i, 6)),
            pl.BlockSpec((rb, H_B * DK_B), lambda b, i: (b * nb + i, 7)),
            pl.BlockSpec((rb, H_B * DV_B), lambda b, i: (b * nb + i, 4)),
            pl.BlockSpec((rb, H_B * DV_B), lambda b, i: (b * nb + i, 5)),
            pl.BlockSpec((rb, LANES), lambda b, i: (b * nb + i, 0)),
            pl.BlockSpec((LANES, H_B * DK_B), lambda b, i: (0, 0)),
            pl.BlockSpec((1, H_B * DK_B), lambda b, i: (0, 0)),
            pl.BlockSpec((1, DV_B), lambda b, i: (0, 0)),
        ],
        out_specs=pl.BlockSpec((rb, H_B * DV_B), lambda b, i: (b * nb + i, 0)),
        out_shape=jax.ShapeDtypeStruct((T, H_B * DV_B), BF16),
        scratch_shapes=[pltpu.VMEM((H_B, DV_B, DK_B), F32)],
        compiler_params=_cparams(("parallel", "arbitrary")),
        name="gla",
    )(hp, hp, hp, hp, tail, w2p, b2, norm_g)


GDN_GROUP = 2 * CHUNK


def _gdn_kernel(q_ref, k_ref, v_ref, z_ref, t_ref, cw_ref, hp_ref, g_ref, o_ref, s_sc, xwin_sc, *, rb):
    G = GDN_GROUP

    @pl.when(pl.program_id(1) == 0)
    def _():
        s_sc[...] = jnp.zeros(s_sc.shape, F32)
        xwin_sc[:, 0:8, :] = jnp.zeros((3, 8, H_C * DK_C), F32)

    r = lax.broadcasted_iota(jnp.int32, (G, G), 0)
    c = lax.broadcasted_iota(jnp.int32, (G, G), 1)
    same = (r >= CHUNK) == (c >= CHUNK)
    incl = same & (c <= r)
    strict = same & (c < r)
    tril2 = incl.astype(F32)
    eye = (r == c).astype(F32)

    def conv_silu(idx, h):
        ls = slice(h * LANES, (h + 1) * LANES)
        w = cw_ref[:, idx * H_C * LANES + h * LANES: idx * H_C * LANES + (h + 1) * LANES]
        y = w[CONV_C - 1:CONV_C, :] * xwin_sc[idx, 8:8 + G, ls]
        for j in range(CONV_C - 1):
            y = y + w[j:j + 1, :] * xwin_sc[idx, 5 + j:5 + j + G, ls]
        return y * _sigmoid(y)

    def group(gi, carry):
        r0 = pl.multiple_of(gi * G, G)
        xwin_sc[0, 8:8 + G, :] = q_ref[pl.ds(r0, G), :].astype(F32)
        xwin_sc[1, 8:8 + G, :] = k_ref[pl.ds(r0, G), :].astype(F32)
        xwin_sc[2, 8:8 + G, :] = v_ref[pl.ds(r0, G), :].astype(F32)
        t = t_ref[pl.ds(r0, G), :]
        beta_all = _sigmoid(t)
        g_all = -jnp.exp(hp_ref[0:1, :]) * _softplus(t + hp_ref[1:2, :])
        cum_all = jnp.dot(tril2, g_all, precision=HI, preferred_element_type=F32)
        cum_all_t = cum_all.T
        heads = range(H_C)
        qs = [conv_silu(0, h) for h in heads]
        ks = [conv_silu(1, h) for h in heads]
        vs = [conv_silu(2, h) for h in heads]
        qs = [q * lax.rsqrt(jnp.sum(q * q, axis=-1, keepdims=True) + RMS_EPS) * (DK_C ** -0.5) for q in qs]
        ks = [k * lax.rsqrt(jnp.sum(k * k, axis=-1, keepdims=True) + RMS_EPS) for k in ks]
        bhs = [jnp.broadcast_to(beta_all[:, h:h + 1], (G, LANES)) for h in heads]
        chs = [jnp.broadcast_to(cum_all[:, H_C + h:H_C + h + 1], (G, LANES)) for h in heads]
        crows = [jnp.broadcast_to(cum_all_t[H_C + h:H_C + h + 1, :], (G, G)) for h in heads]
        decs = [jnp.where(incl, jnp.exp(ch - crow), 0.0) for ch, crow in zip(chs, crows)]
        egs = [jnp.exp(ch) for ch in chs]
        kbs = [k * bh for k, bh in zip(ks, bhs)]
        khbs = [k.astype(BF16) for k in ks]
        kks = [_dot_nt(kb.astype(BF16), khb) for kb, khb in zip(kbs, khbs)]
        ps = [-jnp.where(strict, kk * dec, 0.0) for kk, dec in zip(kks, decs)]
        tinvs = [eye + p for p in ps]
        for _ in range(5):
            pbs = [p.astype(BF16) for p in ps]
            ps = [_dot(pb, pb) for pb in pbs]
            tinvs = [tinv + _dot(p.astype(BF16), tinv.astype(BF16)) for p, tinv in zip(ps, tinvs)]
        rhss = [jnp.concatenate([v * bh, kb * eg], axis=1).astype(BF16)
                for v, bh, kb, eg in zip(vs, bhs, kbs, egs)]
        uws = [_dot(tinv.astype(BF16), rhs) for tinv, rhs in zip(tinvs, rhss)]
        qks = [_dot_nt(q.astype(BF16), khb) * dec for q, khb, dec in zip(qs, khbs, decs)]
        qgs = [q * eg for q, eg in zip(qs, egs)]
        sts = [s_sc[h] for h in heads]
        outs = [[] for _ in heads]
        for c2 in range(G // CHUNK):
            cs = slice(c2 * CHUNK, (c2 + 1) * CHUNK)
            wss = [_dot(jnp.concatenate([uw[cs, DV_C:], qg[cs]], axis=0).astype(BF16), st.astype(BF16))
                   for uw, qg, st in zip(uws, qgs, sts)]
            v_news = [uw[cs, :DV_C] - ws[:CHUNK] for uw, ws in zip(uws, wss)]
            for h in heads:
                outs[h].append(wss[h][CHUNK:] + _dot(qks[h][cs, cs].astype(BF16), v_news[h].astype(BF16)))
            gls = [ch[(c2 + 1) * CHUNK - 1:(c2 + 1) * CHUNK, :] for ch in chs]
            kgs = [k[cs] * jnp.exp(gl - ch[cs]) for k, gl, ch in zip(ks, gls, chs)]
            sts = [st * jnp.exp(gl) + _dot(kg.T.astype(BF16), v_new.astype(BF16))
                   for st, gl, kg, v_new in zip(sts, gls, kgs, v_news)]
        for h in heads:
            ls = slice(h * LANES, (h + 1) * LANES)
            s_sc[h] = sts[h]
            o = jnp.concatenate(outs[h], axis=0)
            ms = jnp.mean(o * o, axis=-1, keepdims=True)
            z = z_ref[pl.ds(r0, G), ls].astype(F32)
            y = o * lax.rsqrt(ms + RMS_EPS) * g_ref[...] * (z * _sigmoid(z))
            o_ref[pl.ds(r0, G), ls] = y.astype(BF16)
        xwin_sc[:, 0:8, :] = xwin_sc[:, G:G + 8, :]
        return carry

    lax.fori_loop(0, rb // G, group, 0)


def _gdn(hp, tail, conv_w, head_par, norm_g, *, B, S, rb):
    nb = S // rb
    T = B * S
    width = H_C * DK_C
    kern = functools.partial(_gdn_kernel, rb=rb)
    return pl.pallas_call(
        kern,
        grid=(B, nb),
        in_specs=[
            pl.BlockSpec((rb, width), lambda b, i: (b * nb + i, 0)),
            pl.BlockSpec((rb, width), lambda b, i: (b * nb + i, 1)),
            pl.BlockSpec((rb, width), lambda b, i: (b * nb + i, 2)),
            pl.BlockSpec((rb, width), lambda b, i: (b * nb + i, 3)),
            pl.BlockSpec((rb, LANES), lambda b, i: (b * nb + i, 0)),
            pl.BlockSpec((CONV_C, 3 * width), lambda b, i: (0, 0)),
            pl.BlockSpec((8, LANES), lambda b, i: (0, 0)),
            pl.BlockSpec((1, DV_C), lambda b, i: (0, 0)),
        ],
        out_specs=pl.BlockSpec((rb, width), lambda b, i: (b * nb + i, 0)),
        out_shape=jax.ShapeDtypeStruct((T, width), BF16),
        scratch_shapes=[
            pltpu.VMEM((H_C, DK_C, DV_C), F32),
            pltpu.VMEM((3, GDN_GROUP + 8, width), F32),
        ],
        compiler_params=_cparams(("parallel", "arbitrary")),
        name="gdn",
    )(hp, hp, hp, hp, tail, conv_w, head_par, norm_g)


def _outproj_ln_kernel(*refs, n_in):
    ins = refs[:n_in]
    w_ref, res_ref, g_ref, b_ref, o_ref, ob_ref = refs[n_in:]
    y = ALPHA * res_ref[...]
    off = 0
    for r in ins:
        kdim = r.shape[1]
        y = y + _dot(r[...], w_ref[off:off + kdim, :])
        off += kdim
    out = _layer_norm(y, g_ref[...], b_ref[...])
    o_ref[...] = out
    ob_ref[...] = out.astype(BF16)


def _outproj_ln(ins, w, res, g, b, *, tm):
    T = res.shape[0]
    kern = functools.partial(_outproj_ln_kernel, n_in=len(ins))
    in_specs = [pl.BlockSpec((tm, a.shape[1]), lambda i: (i, 0)) for a in ins]
    in_specs += [
        _const_spec(w.shape),
        pl.BlockSpec((tm, D_MODEL), lambda i: (i, 0)),
        pl.BlockSpec((1, D_MODEL), lambda i: (0, 0)),
        pl.BlockSpec((1, D_MODEL), lambda i: (0, 0)),
    ]
    return pl.pallas_call(
        kern,
        grid=(T // tm,),
        in_specs=in_specs,
        out_specs=[pl.BlockSpec((tm, D_MODEL), lambda i: (i, 0))] * 2,
        out_shape=[jax.ShapeDtypeStruct((T, D_MODEL), F32), jax.ShapeDtypeStruct((T, D_MODEL), BF16)],
        compiler_params=_cparams(("parallel",)),
        name="outproj_ln",
    )(*ins, w, res, g, b)


def _xattn_kernel(hb_ref, h_ref, kv_ref, wq_ref, wo_ref, g_ref, b_ref, o_ref, ob_ref, att_sc):
    q = (_dot(hb_ref[...], wq_ref[...]) * (D_XA ** -0.5)).astype(BF16)
    for hd in range(N_XA):
        cs = slice(hd * D_XA, (hd + 1) * D_XA)
        s = _dot_nt(q[:, cs], kv_ref[:, cs])
        p = jnp.exp(s - jnp.max(s, axis=-1, keepdims=True))
        l = jnp.sum(p, axis=-1, keepdims=True)
        vs = slice(D_MODEL + hd * D_XA, D_MODEL + (hd + 1) * D_XA)
        att_sc[:, cs] = (_dot(p.astype(BF16), kv_ref[:, vs]) / l).astype(BF16)
    y = ALPHA * h_ref[...] + _dot(att_sc[...], wo_ref[...])
    out = _layer_norm(y, g_ref[...], b_ref[...])
    o_ref[...] = out
    ob_ref[...] = out.astype(BF16)


def _xattn(hb, h, kv, wq, wo, g, b, *, S, tm):
    T = h.shape[0]
    tiles_per_seq = S // tm
    return pl.pallas_call(
        _xattn_kernel,
        grid=(T // tm,),
        in_specs=[
            pl.BlockSpec((tm, D_MODEL), lambda i: (i, 0)),
            pl.BlockSpec((tm, D_MODEL), lambda i: (i, 0)),
            pl.BlockSpec((N_MEM, 2 * D_MODEL), lambda i: (i // tiles_per_seq, 0)),
            _const_spec((D_MODEL, D_MODEL)),
            _const_spec((D_MODEL, D_MODEL)),
            pl.BlockSpec((1, D_MODEL), lambda i: (0, 0)),
            pl.BlockSpec((1, D_MODEL), lambda i: (0, 0)),
        ],
        out_specs=[pl.BlockSpec((tm, D_MODEL), lambda i: (i, 0))] * 2,
        out_shape=[jax.ShapeDtypeStruct((T, D_MODEL), F32), jax.ShapeDtypeStruct((T, D_MODEL), BF16)],
        scratch_shapes=[pltpu.VMEM((tm, D_MODEL), BF16)],
        compiler_params=_cparams(("parallel",)),
        name="xattn",
    )(hb, h, kv, wq, wo, g, b)


def _ffn_kernel(hprev_ref, hb_ref, h_ref, win_ref, cw_ref, cb_ref, wout_ref, g_ref, b_ref, o_ref, ob_ref,
                xcat_sc, hid_sc, act_sc, *, tm, tiles_per_seq):
    is_start = (pl.program_id(0) % tiles_per_seq) == 0
    prev = hprev_ref[...]
    xcat_sc[0:HALO, :] = jnp.where(is_start, jnp.zeros_like(prev), prev)
    xcat_sc[HALO:HALO + tm, :] = hb_ref[...]
    xc = xcat_sc[...]

    def conv(c):
        hid_sc[...] = _dot(xc, win_ref[c])
        w = cw_ref[c]
        y = cb_ref[c] + w[FFN_CONV - 1:FFN_CONV, :] * hid_sc[HALO:HALO + tm, :]
        for j in range(FFN_CONV - 1):
            off = HALO - (FFN_CONV - 1) + j
            y = y + w[j:j + 1, :] * hid_sc[off:off + tm, :]
        return y

    for c in range(N_FF_CHUNKS):
        gate = conv(c)
        up = conv(N_FF_CHUNKS + c)
        act_sc[:, c * FF_CHUNK:(c + 1) * FF_CHUNK] = (gate * _sigmoid(gate) * up).astype(BF16)
    y = ALPHA * h_ref[...] + _dot(act_sc[...], wout_ref[...])
    out = _layer_norm(y, g_ref[...], b_ref[...])
    o_ref[...] = out
    ob_ref[...] = out.astype(BF16)


def _ffn(hb, h, win_r, cw_r, cb_r, wout, g, b, *, S, tm):
    T = h.shape[0]
    tiles_per_seq = S // tm
    kern = functools.partial(_ffn_kernel, tm=tm, tiles_per_seq=tiles_per_seq)
    halo_blocks = tm // HALO
    return pl.pallas_call(
        kern,
        grid=(T // tm,),
        in_specs=[
            pl.BlockSpec((HALO, D_MODEL), lambda i: (jnp.maximum(i * halo_blocks - 1, 0), 0)),
            pl.BlockSpec((tm, D_MODEL), lambda i: (i, 0)),
            pl.BlockSpec((tm, D_MODEL), lambda i: (i, 0)),
            _const_spec(win_r.shape),
            _const_spec(cw_r.shape),
            _const_spec(cb_r.shape),
            _const_spec(wout.shape),
            pl.BlockSpec((1, D_MODEL), lambda i: (0, 0)),
            pl.BlockSpec((1, D_MODEL), lambda i: (0, 0)),
        ],
        out_specs=[pl.BlockSpec((tm, D_MODEL), lambda i: (i, 0))] * 2,
        out_shape=[jax.ShapeDtypeStruct((T, D_MODEL), F32), jax.ShapeDtypeStruct((T, D_MODEL), BF16)],
        scratch_shapes=[
            pltpu.VMEM((tm + HALO, D_MODEL), BF16),
            pltpu.VMEM((tm + HALO, FF_CHUNK), F32),
            pltpu.VMEM((tm, D_FF), BF16),
        ],
        compiler_params=_cparams(("parallel",)),
        name="ffn",
    )(hb, hb, h, win_r, cw_r, cb_r, wout, g, b)


def _rope_consts():
    inv_freq = ROPE_THETA ** (-jnp.arange(0, ROPE_DIM, 2, dtype=F32) / ROPE_DIM)
    d = jnp.arange(LANES) % D_A
    half = ROPE_DIM // 2
    freq = jnp.where(d < ROPE_DIM, inv_freq[d % half], 0.0)
    lo = jnp.where(d < half, -1.0, 0.0)
    hi = jnp.where((d >= half) & (d < ROPE_DIM), 1.0, 0.0)
    return jnp.zeros((8, LANES), F32).at[0].set(freq).at[1].set(lo).at[2].set(hi)


def _row(v):
    return v.reshape(1, -1).astype(F32)


def _tail_w(w, n_main):
    tail = w[:, n_main:]
    return jnp.pad(tail, ((0, 0), (0, LANES - tail.shape[1]))).astype(BF16)


def _ffn_params(f_in, f_cw, f_cb, f_out):
    n2 = 2 * N_FF_CHUNKS
    win_r = f_in.astype(BF16).reshape(D_MODEL, n2, FF_CHUNK).transpose(1, 0, 2)
    cw_r = f_cw.reshape(FFN_CONV, n2, FF_CHUNK).transpose(1, 0, 2)
    cb_r = f_cb.reshape(n2, 1, FF_CHUNK)
    return win_r, cw_r, cb_r, f_out.astype(BF16)


def _common(h, hb, mem2d, S, params, tm):
    (ln2_g, ln2_b, xa_wq, xa_wkv, xa_wo, f_in, f_cw, f_cb, f_out, ln3_g, ln3_b) = params
    kv = _kvproj(mem2d, xa_wkv.astype(BF16), tm=tm)
    h, hb = _xattn(hb, h, kv, xa_wq.astype(BF16), xa_wo.astype(BF16), _row(ln2_g), _row(ln2_b), S=S, tm=tm)
    win_r, cw_r, cb_r, wout = _ffn_params(f_in, f_cw, f_cb, f_out)
    return _ffn(hb, h, win_r, cw_r, cb_r, wout, _row(ln3_g), _row(ln3_b), S=S, tm=tm)


def kernel(x, mem, positions, w_in_0, diff_lambda_0, diff_norm_0, gla_w2_0, gla_b2_0, gla_norm_0, w_mix_out_0, ln1_g_0, ln1_b_0, xa_wq_0, xa_wkv_0, xa_wo_0, ln2_g_0, ln2_b_0, ffn_w_in_0, ffn_conv_w_0, ffn_conv_b_0, ffn_w_out_0, ln3_g_0, ln3_b_0, w_in_1, gdn_conv_w_1, gdn_a_log_1, gdn_dt_bias_1, gdn_norm_1, w_mix_out_1, ln1_g_1, ln1_b_1, xa_wq_1, xa_wkv_1, xa_wo_1, ln2_g_1, ln2_b_1, ffn_w_in_1, ffn_conv_w_1, ffn_conv_b_1, ffn_w_out_1, ln3_g_1, ln3_b_1):
    B, S, _ = x.shape
    T = B * S
    tm = min(512, S)
    x2d = x.reshape(T, D_MODEL)
    mem2d = mem.reshape(B * N_MEM, D_MODEL)

    hp, tail = _proj0(x2d, positions.reshape(T, 1), _rope_consts(),
                      w_in_0[:, :P0_MAIN].astype(BF16), _tail_w(w_in_0, P0_MAIN), tm=tm)
    lam_init = 0.8 - 0.6 * math.exp(-0.3 * 0)
    oa = _diffattn(hp, diff_lambda_0.astype(F32), _row(diff_norm_0), B=B, S=S, tq=min(256, S), lam_init=lam_init)
    w2p = jnp.pad(gla_w2_0.astype(F32), ((0, LANES - GATE_RANK), (0, 0)))
    ob = _gla(hp, tail, w2p, _row(gla_b2_0), _row(gla_norm_0), B=B, S=S, rb=min(256, S))
    h, hb = _outproj_ln([oa, ob], w_mix_out_0.astype(BF16), x2d, _row(ln1_g_0), _row(ln1_b_0), tm=tm)
    h, hb = _common(h, hb, mem2d, S,
                    (ln2_g_0, ln2_b_0, xa_wq_0, xa_wkv_0, xa_wo_0, ffn_w_in_0, ffn_conv_w_0, ffn_conv_b_0,
                     ffn_w_out_0, ln3_g_0, ln3_b_0), tm)

    hp, tail = _proj1(hb, w_in_1[:, :P1_MAIN].astype(BF16), _tail_w(w_in_1, P1_MAIN), tm=tm)
    head_par = (jnp.zeros((8, LANES), F32)
                .at[0, H_C:2 * H_C].set(gdn_a_log_1.astype(F32))
                .at[1, H_C:2 * H_C].set(gdn_dt_bias_1.astype(F32)))
    oc = _gdn(hp, tail, gdn_conv_w_1.astype(F32), head_par, _row(gdn_norm_1), B=B, S=S, rb=min(512, S))
    h, hb = _outproj_ln([oc], w_mix_out_1.astype(BF16), h, _row(ln1_g_1), _row(ln1_b_1), tm=tm)
    h, hb = _common(h, hb, mem2d, S,
                    (ln2_g_1, ln2_b_1, xa_wq_1, xa_wkv_1, xa_wo_1, ffn_w_in_1, ffn_conv_w_1, ffn_conv_b_1,
                     ffn_w_out_1, ln3_g_1, ln3_b_1), tm)
    return h.reshape(B, S, D_MODEL)
```

```python
import functools
import math

import jax
import jax.numpy as jnp
from jax import lax
from jax.experimental import pallas as pl
from jax.experimental.pallas import tpu as pltpu

F32 = jnp.float32
BF16 = jnp.bfloat16

D_MODEL = 1024
N_MEM = 256
H_A, D_A, DV_A = 4, 64, 128
ROPE_THETA, ROPE_DIM = 500000.0, 16
H_B, DK_B, DV_B = 4, 64, 128
GATE_RANK, GATE_NORMALIZER = 16, 16.0
H_C, DK_C, DV_C, CONV_C = 8, 128, 128, 4
CHUNK = 64
N_XA, D_XA = 4, 256
D_FF = 2816
FFN_CONV = 3
LN_EPS, RMS_EPS = 1e-5, 1e-6
DEPTH = 2
ALPHA = (2.0 * DEPTH) ** 0.25
P0_MAIN = 3072
P1_MAIN = 4096
LANES = 128
FF_CHUNK = 256
N_FF_CHUNKS = D_FF // FF_CHUNK
HALO = 16
VMEM_LIMIT = 56 * 1024 * 1024

NT_DIMS = (((1,), (1,)), ((), ()))
TN_DIMS = (((0,), (0,)), ((), ()))
HI = lax.Precision.HIGHEST


def _cparams(sem):
    return pltpu.CompilerParams(dimension_semantics=sem, vmem_limit_bytes=VMEM_LIMIT)


def _const_spec(shape):
    nd = len(shape)
    return pl.BlockSpec(shape, lambda *_: (0,) * nd, pipeline_mode=pl.Buffered(1))


def _dot(a, b):
    return jnp.dot(a, b, preferred_element_type=F32)


def _dot_nt(a, b, precision=None):
    return lax.dot_general(a, b, NT_DIMS, precision=precision, preferred_element_type=F32)


def _layer_norm(y, g, b):
    mu = jnp.mean(y, axis=-1, keepdims=True)
    yc = y - mu
    var = jnp.mean(yc * yc, axis=-1, keepdims=True)
    return yc * lax.rsqrt(var + LN_EPS) * g + b


def _sigmoid(x):
    return 1.0 / (1.0 + jnp.exp(-x))


def _softplus(x):
    return jnp.maximum(x, 0.0) + jnp.log(1.0 + jnp.exp(-jnp.abs(x)))


def _tril_mask(n, strict=False):
    r = lax.broadcasted_iota(jnp.int32, (n, n), 0)
    c = lax.broadcasted_iota(jnp.int32, (n, n), 1)
    return (c < r) if strict else (c <= r)


def _proj0_kernel(x_ref, pos_ref, rc_ref, w_ref, wt_ref, o_ref, t_ref):
    xb = x_ref[...].astype(BF16)
    ang = pos_ref[...].astype(F32) * rc_ref[0:1, :]
    c = jnp.cos(ang)
    s = jnp.sin(ang)
    s_lo = s * rc_ref[1:2, :]
    s_hi = s * rc_ref[2:3, :]
    q_scale = D_A ** -0.5
    for j in range(2):
        y = _dot(xb, w_ref[:, j * 512:(j + 1) * 512])
        for t in range(4):
            yt = y[:, t * LANES:(t + 1) * LANES]
            yt = yt * c + pltpu.roll(yt, LANES - 8, 1) * s_lo + pltpu.roll(yt, 8, 1) * s_hi
            if j == 0:
                yt = yt * q_scale
            col = j * 512 + t * LANES
            o_ref[:, col:col + LANES] = yt.astype(BF16)
    for col in range(1024, P0_MAIN, 512):
        o_ref[:, col:col + 512] = _dot(xb, w_ref[:, col:col + 512]).astype(BF16)
    t_ref[...] = _dot(xb, wt_ref[...])


def _proj0(x2d, pos2d, rope_c, w_main, w_tail, *, tm):
    T = x2d.shape[0]
    return pl.pallas_call(
        _proj0_kernel,
        grid=(T // tm,),
        in_specs=[
            pl.BlockSpec((tm, D_MODEL), lambda i: (i, 0)),
            pl.BlockSpec((tm, 1), lambda i: (i, 0)),
            _const_spec((8, LANES)),
            _const_spec((D_MODEL, P0_MAIN)),
            _const_spec((D_MODEL, LANES)),
        ],
        out_specs=[
            pl.BlockSpec((tm, P0_MAIN), lambda i: (i, 0)),
            pl.BlockSpec((tm, LANES), lambda i: (i, 0)),
        ],
        out_shape=[
            jax.ShapeDtypeStruct((T, P0_MAIN), BF16),
            jax.ShapeDtypeStruct((T, LANES), F32),
        ],
        compiler_params=_cparams(("parallel",)),
        name="proj0",
    )(x2d, pos2d, rope_c, w_main, w_tail)


def _proj1_kernel(x_ref, w_ref, wt_ref, o_ref, t_ref):
    xb = x_ref[...]
    for col in range(0, P1_MAIN, 512):
        o_ref[:, col:col + 512] = _dot(xb, w_ref[:, col:col + 512]).astype(BF16)
    t_ref[...] = _dot(xb, wt_ref[...])


def _proj1(xb2d, w_main, w_tail, *, tm):
    T = xb2d.shape[0]
    return pl.pallas_call(
        _proj1_kernel,
        grid=(T // tm,),
        in_specs=[
            pl.BlockSpec((tm, D_MODEL), lambda i: (i, 0)),
            _const_spec((D_MODEL, P1_MAIN)),
            _const_spec((D_MODEL, LANES)),
        ],
        out_specs=[
            pl.BlockSpec((tm, P1_MAIN), lambda i: (i, 0)),
            pl.BlockSpec((tm, LANES), lambda i: (i, 0)),
        ],
        out_shape=[
            jax.ShapeDtypeStruct((T, P1_MAIN), BF16),
            jax.ShapeDtypeStruct((T, LANES), F32),
        ],
        compiler_params=_cparams(("parallel",)),
        name="proj1",
    )(xb2d, w_main, w_tail)


def _kvproj_kernel(m_ref, w_ref, o_ref):
    mb = m_ref[...].astype(BF16)
    for col in range(0, 2 * D_MODEL, 512):
        o_ref[:, col:col + 512] = _dot(mb, w_ref[:, col:col + 512]).astype(BF16)


def _kvproj(mem2d, wkv, *, tm):
    R = mem2d.shape[0]
    return pl.pallas_call(
        _kvproj_kernel,
        grid=(R // tm,),
        in_specs=[
            pl.BlockSpec((tm, D_MODEL), lambda i: (i, 0)),
            _const_spec((D_MODEL, 2 * D_MODEL)),
        ],
        out_specs=pl.BlockSpec((tm, 2 * D_MODEL), lambda i: (i, 0)),
        out_shape=jax.ShapeDtypeStruct((R, 2 * D_MODEL), BF16),
        compiler_params=_cparams(("parallel",)),
        name="kvproj",
    )(mem2d, wkv)


def _diffattn_kernel(q_ref, k_ref, v_ref, lam_ref, g_ref, o_ref, *, S, tq, lam_init):
    lv = lam_ref[...]
    lam = (jnp.exp(jnp.sum(lv[0:1] * lv[1:2], axis=-1, keepdims=True))
           - jnp.exp(jnp.sum(lv[2:3] * lv[3:4], axis=-1, keepdims=True)) + lam_init)
    lane = lax.broadcasted_iota(jnp.int32, (tq, LANES), 1)
    r = lax.broadcasted_iota(jnp.int32, (2 * tq, tq), 0)
    c = lax.broadcasted_iota(jnp.int32, (2 * tq, tq), 1)
    causal = c <= jnp.where(r >= tq, r - tq, r)
    for qi in range(S // tq):
        n_keys = (qi + 1) * tq
        q = q_ref[qi * tq:(qi + 1) * tq, :]
        zero = jnp.zeros_like(q)
        qq = jnp.concatenate([jnp.where(lane < D_A, q, zero), jnp.where(lane >= D_A, q, zero)], axis=0)
        s = _dot_nt(qq, k_ref[0:n_keys, :])
        s_diag = jnp.where(causal, s[:, n_keys - tq:], -jnp.inf)
        s = s_diag if qi == 0 else jnp.concatenate([s[:, :n_keys - tq], s_diag], axis=1)
        p = jnp.exp(s - jnp.max(s, axis=-1, keepdims=True))
        l = jnp.sum(p, axis=-1, keepdims=True)
        o_all = _dot(p.astype(BF16), v_ref[0:n_keys, :]) / l
        o = o_all[:tq] - lam * o_all[tq:]
        ms = jnp.mean(o * o, axis=-1, keepdims=True)
        y = o * lax.rsqrt(ms + RMS_EPS) * g_ref[...] * (1.0 - lam_init)
        o_ref[qi * tq:(qi + 1) * tq, :] = y.astype(BF16)


def _diffattn(hp, lam_vecs, norm_g, *, B, S, tq, lam_init):
    T = B * S
    kern = functools.partial(_diffattn_kernel, S=S, tq=tq, lam_init=lam_init)
    return pl.pallas_call(
        kern,
        grid=(B, H_A),
        in_specs=[
            pl.BlockSpec((S, LANES), lambda b, h: (b, h)),
            pl.BlockSpec((S, LANES), lambda b, h: (b, H_A + h)),
            pl.BlockSpec((S, LANES), lambda b, h: (b, 2 * H_A + h)),
            pl.BlockSpec((4, D_A), lambda b, h: (0, 0)),
            pl.BlockSpec((1, DV_A), lambda b, h: (0, 0)),
        ],
        out_specs=pl.BlockSpec((S, LANES), lambda b, h: (b, h)),
        out_shape=jax.ShapeDtypeStruct((T, H_A * DV_A), BF16),
        compiler_params=_cparams(("parallel", "parallel")),
        name="diffattn",
    )(hp, hp, hp, lam_vecs, norm_g)


GLA_GROUP = 2 * CHUNK
GLA_LEVELS = (32, 16, 8, 4, 2, 1)


def _split_dot(m_bf16, hi, lo):
    return _dot(m_bf16, hi) + _dot(m_bf16, lo)


def _gla_kernel(q_ref, k_ref, v_ref, r_ref, t_ref, w2_ref, b2_ref, g_ref, o_ref, st_sc, *, rb):
    G = GLA_GROUP

    @pl.when(pl.program_id(1) == 0)
    def _():
        st_sc[...] = jnp.zeros(st_sc.shape, F32)

    r = lax.broadcasted_iota(jnp.int32, (G, G), 0)
    c = lax.broadcasted_iota(jnp.int32, (G, G), 1)
    tril2 = jnp.where(((r >= CHUNK) == (c >= CHUNK)) & (c <= r), 1.0, 0.0).astype(BF16)
    row = lax.broadcasted_iota(jnp.int32, (G, H_B * DK_B), 0)
    lane = lax.broadcasted_iota(jnp.int32, (G, LANES), 1)
    head_lanes = [lane < DK_B, lane >= DK_B]
    lane_c = lax.broadcasted_iota(jnp.int32, (CHUNK, LANES), 1)
    head_lanes_c = [lane_c < DK_B, lane_c >= DK_B]
    levels = []
    for w in GLA_LEVELS:
        same = (r // (2 * w)) == (c // (2 * w))
        r_in_i = (r % (2 * w)) >= w
        c_in_i = (c % (2 * w)) >= w
        in_range = (r_in_i & c_in_i & (c <= r)) | ((~r_in_i) & (~c_in_i) & (c > r))
        m_arg = jnp.where(same & in_range, 1.0, 0.0).astype(BF16)
        levels.append((m_arg, same, (row % (2 * w)) >= w))
    eye = r == c

    def group(gi, carry):
        r0 = pl.multiple_of(gi * G, G)
        rows = pl.ds(r0, G)
        x = _dot(t_ref[rows, :].astype(BF16), w2_ref[...]) + b2_ref[...]
        gk = (jnp.minimum(x, 0.0) - jnp.log(1.0 + jnp.exp(-jnp.abs(x)))) * (1.0 / GATE_NORMALIZER)
        hi = gk.astype(BF16)
        lo = (gk - hi.astype(F32)).astype(BF16)
        b = _split_dot(tril2, hi, lo)
        q = q_ref[rows, :].astype(F32) * (DK_B ** -0.5)
        k = k_ref[rows, :].astype(F32)
        v = v_ref[rows, :]
        q_lv, k_lv, masks = [], [], []
        for m_arg, same, row_in_i in levels:
            e = jnp.exp(_split_dot(m_arg, hi, lo))
            q_lv.append(jnp.where(row_in_i, q * e, 0.0))
            k_lv.append(jnp.where(row_in_i, 0.0, k * e).astype(BF16))
            masks.append(same)
        q_lv.append(q)
        k_lv.append(k.astype(BF16))
        masks.append(eye)
        atts = [None] * H_B
        for ql, kl_, mask in zip(q_lv, k_lv, masks):
            for h in range(H_B):
                blk = slice((h // 2) * LANES, (h // 2 + 1) * LANES)
                qm = jnp.where(head_lanes[h % 2], ql[:, blk], 0.0).astype(BF16)
                part = jnp.where(mask, _dot_nt(qm, kl_[:, blk]), 0.0)
                atts[h] = part if atts[h] is None else atts[h] + part
        avs = [_dot(atts[h].astype(BF16), v[:, h * DV_B:(h + 1) * DV_B]) for h in range(H_B)]
        qe = q * jnp.exp(b)
        sts = [st_sc[p] for p in range(H_B // 2)]
        inters = [[] for _ in range(H_B)]
        for c2 in range(G // CHUNK):
            cs = slice(c2 * CHUNK, (c2 + 1) * CHUNK)
            for h in range(H_B):
                blk = slice((h // 2) * LANES, (h // 2 + 1) * LANES)
                qm = jnp.where(head_lanes_c[h % 2], qe[cs, blk], 0.0).astype(BF16)
                inters[h].append(_dot_nt(qm, sts[h // 2].astype(BF16)))
            bl = b[(c2 + 1) * CHUNK - 1:(c2 + 1) * CHUNK, :]
            kl = (k[cs] * jnp.exp(bl - b[cs])).astype(BF16)
            ebl = jnp.exp(bl)
            kvs = [lax.dot_general(v[cs, h * DV_B:(h + 1) * DV_B], kl[:, (h // 2) * LANES:(h // 2 + 1) * LANES],
                                   TN_DIMS, preferred_element_type=F32) for h in range(H_B)]
            sts = [sts[p] * ebl[:, p * LANES:(p + 1) * LANES]
                   + jnp.where(head_lanes[0], kvs[2 * p], kvs[2 * p + 1]) for p in range(H_B // 2)]
        for p in range(H_B // 2):
            st_sc[p] = sts[p]
        for h in range(H_B):
            vs = slice(h * DV_B, (h + 1) * DV_B)
            o = jnp.concatenate(inters[h], axis=0) + avs[h]
            ms = jnp.mean(o * o, axis=-1, keepdims=True)
            gate = r_ref[rows, vs].astype(F32)
            y = o * lax.rsqrt(ms + RMS_EPS) * g_ref[...] * (gate * _sigmoid(gate))
            o_ref[rows, vs] = y.astype(BF16)
        return carry

    lax.fori_loop(0, rb // G, group, 0)


def _gla(hp, tail, w2p, b2, norm_g, *, B, S, rb):
    nb = S // rb
    T = B * S
    kern = functools.partial(_gla_kernel, rb=rb)
    return pl.pallas_call(
        kern,
        grid=(B, nb),
        in_specs=[
            pl.BlockSpec((rb, H_B * DK_B), lambda b, i: (b * nb + i, 6)),
            pl.BlockSpec((rb, H_B * DK_B), lambda b, i: (b * nb + i, 7)),
            pl.BlockSpec((rb, H_B * DV_B), lambda b, i: (b * nb + i, 4)),
            pl.BlockSpec((rb, H_B * DV_B), lambda b, i: (b * nb + i, 5)),
            pl.BlockSpec((rb, LANES), lambda b, i: (b * nb + i, 0)),
            pl.BlockSpec((LANES, H_B * DK_B), lambda b, i: (0, 0)),
            pl.BlockSpec((1, H_B * DK_B), lambda b, i: (0, 0)),
            pl.BlockSpec((1, DV_B), lambda b, i: (0, 0)),
        ],
        out_specs=pl.BlockSpec((rb, H_B * DV_B), lambda b, i: (b * nb + i, 0)),
        out_shape=jax.ShapeDtypeStruct((T, H_B * DV_B), BF16),
        scratch_shapes=[pltpu.VMEM((H_B // 2, DV_B, 2 * DK_B), F32)],
        compiler_params=_cparams(("parallel", "arbitrary")),
        name="gla",
    )(hp, hp, hp, hp, tail, w2p, b2, norm_g)


GDN_GROUP = 2 * CHUNK


def _gdn_kernel(q_ref, k_ref, v_ref, z_ref, t_ref, cw_ref, hp_ref, g_ref, o_ref, s_sc, xwin_sc, *, rb):
    G = GDN_GROUP

    @pl.when(pl.program_id(1) == 0)
    def _():
        s_sc[...] = jnp.zeros(s_sc.shape, F32)
        xwin_sc[:, 0:8, :] = jnp.zeros((3, 8, H_C * DK_C), F32)

    r = lax.broadcasted_iota(jnp.int32, (G, G), 0)
    c = lax.broadcasted_iota(jnp.int32, (G, G), 1)
    same = (r >= CHUNK) == (c >= CHUNK)
    incl = same & (c <= r)
    strict = same & (c < r)
    tril2 = incl.astype(F32)
    eye = (r == c).astype(F32)

    def conv_silu(idx, h):
        ls = slice(h * LANES, (h + 1) * LANES)
        w = cw_ref[:, idx * H_C * LANES + h * LANES: idx * H_C * LANES + (h + 1) * LANES]
        y = w[CONV_C - 1:CONV_C, :] * xwin_sc[idx, 8:8 + G, ls]
        for j in range(CONV_C - 1):
            y = y + w[j:j + 1, :] * xwin_sc[idx, 5 + j:5 + j + G, ls]
        return y * _sigmoid(y)

    def group(gi, carry):
        r0 = pl.multiple_of(gi * G, G)
        xwin_sc[0, 8:8 + G, :] = q_ref[pl.ds(r0, G), :].astype(F32)
        xwin_sc[1, 8:8 + G, :] = k_ref[pl.ds(r0, G), :].astype(F32)
        xwin_sc[2, 8:8 + G, :] = v_ref[pl.ds(r0, G), :].astype(F32)
        t = t_ref[pl.ds(r0, G), :]
        beta_all = _sigmoid(t)
        g_all = -jnp.exp(hp_ref[0:1, :]) * _softplus(t + hp_ref[1:2, :])
        cum_all = jnp.dot(tril2, g_all, precision=HI, preferred_element_type=F32)
        cum_all_t = cum_all.T
        heads = range(H_C)
        qs = [conv_silu(0, h) for h in heads]
        ks = [conv_silu(1, h) for h in heads]
        vs = [conv_silu(2, h) for h in heads]
        qs = [q * lax.rsqrt(jnp.sum(q * q, axis=-1, keepdims=True) + RMS_EPS) * (DK_C ** -0.5) for q in qs]
        ks = [k * lax.rsqrt(jnp.sum(k * k, axis=-1, keepdims=True) + RMS_EPS) for k in ks]
        bhs = [jnp.broadcast_to(beta_all[:, h:h + 1], (G, LANES)) for h in heads]
        chs = [jnp.broadcast_to(cum_all[:, H_C + h:H_C + h + 1], (G, LANES)) for h in heads]
        crows = [jnp.broadcast_to(cum_all_t[H_C + h:H_C + h + 1, :], (G, G)) for h in heads]
        decs = [jnp.where(incl, jnp.exp(ch - crow), 0.0) for ch, crow in zip(chs, crows)]
        egs = [jnp.exp(ch) for ch in chs]
        kbs = [k * bh for k, bh in zip(ks, bhs)]
        khbs = [k.astype(BF16) for k in ks]
        kks = [_dot_nt(kb.astype(BF16), khb) for kb, khb in zip(kbs, khbs)]
        ps = [-jnp.where(strict, kk * dec, 0.0) for kk, dec in zip(kks, decs)]
        tinvs = [eye + p for p in ps]
        for _ in range(5):
            pbs = [p.astype(BF16) for p in ps]
            ps = [_dot(pb, pb) for pb in pbs]
            tinvs = [tinv + _dot(p.astype(BF16), tinv.astype(BF16)) for p, tinv in zip(ps, tinvs)]
        rhss = [jnp.concatenate([v * bh, kb * eg], axis=1).astype(BF16)
                for v, bh, kb, eg in zip(vs, bhs, kbs, egs)]
        uws = [_dot(tinv.astype(BF16), rhs) for tinv, rhs in zip(tinvs, rhss)]
        qks = [_dot_nt(q.astype(BF16), khb) * dec for q, khb, dec in zip(qs, khbs, decs)]
        qgs = [q * eg for q, eg in zip(qs, egs)]
        sts = [s_sc[h] for h in heads]
        outs = [[] for _ in heads]
        for c2 in range(G // CHUNK):
            cs = slice(c2 * CHUNK, (c2 + 1) * CHUNK)
            wss = [_dot(jnp.concatenate([uw[cs, DV_C:], qg[cs]], axis=0).astype(BF16), st.astype(BF16))
                   for uw, qg, st in zip(uws, qgs, sts)]
            v_news = [uw[cs, :DV_C] - ws[:CHUNK] for uw, ws in zip(uws, wss)]
            for h in heads:
                outs[h].append(wss[h][CHUNK:] + _dot(qks[h][cs, cs].astype(BF16), v_news[h].astype(BF16)))
            gls = [ch[(c2 + 1) * CHUNK - 1:(c2 + 1) * CHUNK, :] for ch in chs]
            kgs = [k[cs] * jnp.exp(gl - ch[cs]) for k, gl, ch in zip(ks, gls, chs)]
            sts = [st * jnp.exp(gl) + _dot(kg.T.astype(BF16), v_new.astype(BF16))
                   for st, gl, kg, v_new in zip(sts, gls, kgs, v_news)]
        for h in heads:
            ls = slice(h * LANES, (h + 1) * LANES)
            s_sc[h] = sts[h]
            o = jnp.concatenate(outs[h], axis=0)
            ms = jnp.mean(o * o, axis=-1, keepdims=True)
            z = z_ref[pl.ds(r0, G), ls].astype(F32)
            y = o * lax.rsqrt(ms + RMS_EPS) * g_ref[...] * (z * _sigmoid(z))
            o_ref[pl.ds(r0, G), ls] = y.astype(BF16)
        xwin_sc[:, 0:8, :] = xwin_sc[:, G:G + 8, :]
        return carry

    lax.fori_loop(0, rb // G, group, 0)


def _gdn(hp, tail, conv_w, head_par, norm_g, *, B, S, rb):
    nb = S // rb
    T = B * S
    width = H_C * DK_C
    kern = functools.partial(_gdn_kernel, rb=rb)
    return pl.pallas_call(
        kern,
        grid=(B, nb),
        in_specs=[
            pl.BlockSpec((rb, width), lambda b, i: (b * nb + i, 0)),
            pl.BlockSpec((rb, width), lambda b, i: (b * nb + i, 1)),
            pl.BlockSpec((rb, width), lambda b, i: (b * nb + i, 2)),
            pl.BlockSpec((rb, width), lambda b, i: (b * nb + i, 3)),
            pl.BlockSpec((rb, LANES), lambda b, i: (b * nb + i, 0)),
            pl.BlockSpec((CONV_C, 3 * width), lambda b, i: (0, 0)),
            pl.BlockSpec((8, LANES), lambda b, i: (0, 0)),
            pl.BlockSpec((1, DV_C), lambda b, i: (0, 0)),
        ],
        out_specs=pl.BlockSpec((rb, width), lambda b, i: (b * nb + i, 0)),
        out_shape=jax.ShapeDtypeStruct((T, width), BF16),
        scratch_shapes=[
            pltpu.VMEM((H_C, DK_C, DV_C), F32),
            pltpu.VMEM((3, GDN_GROUP + 8, width), F32),
        ],
        compiler_params=_cparams(("parallel", "arbitrary")),
        name="gdn",
    )(hp, hp, hp, hp, tail, conv_w, head_par, norm_g)


def _outproj_ln_kernel(*refs, n_in):
    ins = refs[:n_in]
    w_ref, res_ref, g_ref, b_ref, o_ref, ob_ref = refs[n_in:]
    y = ALPHA * res_ref[...]
    off = 0
    for r in ins:
        kdim = r.shape[1]
        y = y + _dot(r[...], w_ref[off:off + kdim, :])
        off += kdim
    out = _layer_norm(y, g_ref[...], b_ref[...])
    o_ref[...] = out
    ob_ref[...] = out.astype(BF16)


def _outproj_ln(ins, w, res, g, b, *, tm):
    T = res.shape[0]
    kern = functools.partial(_outproj_ln_kernel, n_in=len(ins))
    in_specs = [pl.BlockSpec((tm, a.shape[1]), lambda i: (i, 0)) for a in ins]
    in_specs += [
        _const_spec(w.shape),
        pl.BlockSpec((tm, D_MODEL), lambda i: (i, 0)),
        pl.BlockSpec((1, D_MODEL), lambda i: (0, 0)),
        pl.BlockSpec((1, D_MODEL), lambda i: (0, 0)),
    ]
    return pl.pallas_call(
        kern,
        grid=(T // tm,),
        in_specs=in_specs,
        out_specs=[pl.BlockSpec((tm, D_MODEL), lambda i: (i, 0))] * 2,
        out_shape=[jax.ShapeDtypeStruct((T, D_MODEL), F32), jax.ShapeDtypeStruct((T, D_MODEL), BF16)],
        compiler_params=_cparams(("parallel",)),
        name="outproj_ln",
    )(*ins, w, res, g, b)


def _xattn_kernel(hb_ref, h_ref, kv_ref, wq_ref, wo_ref, g_ref, b_ref, o_ref, ob_ref, att_sc):
    q = (_dot(hb_ref[...], wq_ref[...]) * (D_XA ** -0.5)).astype(BF16)
    for hd in range(N_XA):
        cs = slice(hd * D_XA, (hd + 1) * D_XA)
        s = _dot_nt(q[:, cs], kv_ref[:, cs])
        p = jnp.exp(s - jnp.max(s, axis=-1, keepdims=True))
        l = jnp.sum(p, axis=-1, keepdims=True)
        vs = slice(D_MODEL + hd * D_XA, D_MODEL + (hd + 1) * D_XA)
        att_sc[:, cs] = (_dot(p.astype(BF16), kv_ref[:, vs]) / l).astype(BF16)
    y = ALPHA * h_ref[...] + _dot(att_sc[...], wo_ref[...])
    out = _layer_norm(y, g_ref[...], b_ref[...])
    o_ref[...] = out
    ob_ref[...] = out.astype(BF16)


def _xattn(hb, h, kv, wq, wo, g, b, *, S, tm):
    T = h.shape[0]
    tiles_per_seq = S // tm
    return pl.pallas_call(
        _xattn_kernel,
        grid=(T // tm,),
        in_specs=[
            pl.BlockSpec((tm, D_MODEL), lambda i: (i, 0)),
            pl.BlockSpec((tm, D_MODEL), lambda i: (i, 0)),
            pl.BlockSpec((N_MEM, 2 * D_MODEL), lambda i: (i // tiles_per_seq, 0)),
            _const_spec((D_MODEL, D_MODEL)),
            _const_spec((D_MODEL, D_MODEL)),
            pl.BlockSpec((1, D_MODEL), lambda i: (0, 0)),
            pl.BlockSpec((1, D_MODEL), lambda i: (0, 0)),
        ],
        out_specs=[pl.BlockSpec((tm, D_MODEL), lambda i: (i, 0))] * 2,
        out_shape=[jax.ShapeDtypeStruct((T, D_MODEL), F32), jax.ShapeDtypeStruct((T, D_MODEL), BF16)],
        scratch_shapes=[pltpu.VMEM((tm, D_MODEL), BF16)],
        compiler_params=_cparams(("parallel",)),
        name="xattn",
    )(hb, h, kv, wq, wo, g, b)


def _ffn_kernel(hprev_ref, hb_ref, h_ref, win_ref, cw_ref, cb_ref, wout_ref, g_ref, b_ref, o_ref, ob_ref,
                xcat_sc, hid_sc, act_sc, *, tm, tiles_per_seq):
    is_start = (pl.program_id(0) % tiles_per_seq) == 0
    prev = hprev_ref[...]
    xcat_sc[0:HALO, :] = jnp.where(is_start, jnp.zeros_like(prev), prev)
    xcat_sc[HALO:HALO + tm, :] = hb_ref[...]
    xc = xcat_sc[...]

    def conv(c):
        hid_sc[...] = _dot(xc, win_ref[c])
        w = cw_ref[c]
        y = cb_ref[c] + w[FFN_CONV - 1:FFN_CONV, :] * hid_sc[HALO:HALO + tm, :]
        for j in range(FFN_CONV - 1):
            off = HALO - (FFN_CONV - 1) + j
            y = y + w[j:j + 1, :] * hid_sc[off:off + tm, :]
        return y

    for c in range(N_FF_CHUNKS):
        gate = conv(c)
        up = conv(N_FF_CHUNKS + c)
        act_sc[:, c * FF_CHUNK:(c + 1) * FF_CHUNK] = (gate * _sigmoid(gate) * up).astype(BF16)
    y = ALPHA * h_ref[...] + _dot(act_sc[...], wout_ref[...])
    out = _layer_norm(y, g_ref[...], b_ref[...])
    o_ref[...] = out
    ob_ref[...] = out.astype(BF16)


def _ffn(hb, h, win_r, cw_r, cb_r, wout, g, b, *, S, tm):
    T = h.shape[0]
    tiles_per_seq = S // tm
    kern = functools.partial(_ffn_kernel, tm=tm, tiles_per_seq=tiles_per_seq)
    halo_blocks = tm // HALO
    return pl.pallas_call(
        kern,
        grid=(T // tm,),
        in_specs=[
            pl.BlockSpec((HALO, D_MODEL), lambda i: (jnp.maximum(i * halo_blocks - 1, 0), 0)),
            pl.BlockSpec((tm, D_MODEL), lambda i: (i, 0)),
            pl.BlockSpec((tm, D_MODEL), lambda i: (i, 0)),
            _const_spec(win_r.shape),
            _const_spec(cw_r.shape),
            _const_spec(cb_r.shape),
            _const_spec(wout.shape),
            pl.BlockSpec((1, D_MODEL), lambda i: (0, 0)),
            pl.BlockSpec((1, D_MODEL), lambda i: (0, 0)),
        ],
        out_specs=[pl.BlockSpec((tm, D_MODEL), lambda i: (i, 0))] * 2,
        out_shape=[jax.ShapeDtypeStruct((T, D_MODEL), F32), jax.ShapeDtypeStruct((T, D_MODEL), BF16)],
        scratch_shapes=[
            pltpu.VMEM((tm + HALO, D_MODEL), BF16),
            pltpu.VMEM((tm + HALO, FF_CHUNK), F32),
            pltpu.VMEM((tm, D_FF), BF16),
        ],
        compiler_params=_cparams(("parallel",)),
        name="ffn",
    )(hb, hb, h, win_r, cw_r, cb_r, wout, g, b)


def _rope_consts():
    inv_freq = ROPE_THETA ** (-jnp.arange(0, ROPE_DIM, 2, dtype=F32) / ROPE_DIM)
    d = jnp.arange(LANES) % D_A
    half = ROPE_DIM // 2
    freq = jnp.where(d < ROPE_DIM, inv_freq[d % half], 0.0)
    lo = jnp.where(d < half, -1.0, 0.0)
    hi = jnp.where((d >= half) & (d < ROPE_DIM), 1.0, 0.0)
    return jnp.zeros((8, LANES), F32).at[0].set(freq).at[1].set(lo).at[2].set(hi)


def _row(v):
    return v.reshape(1, -1).astype(F32)


def _tail_w(w, n_main):
    tail = w[:, n_main:]
    return jnp.pad(tail, ((0, 0), (0, LANES - tail.shape[1]))).astype(BF16)


def _ffn_params(f_in, f_cw, f_cb, f_out):
    n2 = 2 * N_FF_CHUNKS
    win_r = f_in.astype(BF16).reshape(D_MODEL, n2, FF_CHUNK).transpose(1, 0, 2)
    cw_r = f_cw.reshape(FFN_CONV, n2, FF_CHUNK).transpose(1, 0, 2)
    cb_r = f_cb.reshape(n2, 1, FF_CHUNK)
    return win_r, cw_r, cb_r, f_out.astype(BF16)


def _common(h, hb, mem2d, S, params, tm):
    (ln2_g, ln2_b, xa_wq, xa_wkv, xa_wo, f_in, f_cw, f_cb, f_out, ln3_g, ln3_b) = params
    kv = _kvproj(mem2d, xa_wkv.astype(BF16), tm=tm)
    h, hb = _xattn(hb, h, kv, xa_wq.astype(BF16), xa_wo.astype(BF16), _row(ln2_g), _row(ln2_b), S=S, tm=tm)
    win_r, cw_r, cb_r, wout = _ffn_params(f_in, f_cw, f_cb, f_out)
    return _ffn(hb, h, win_r, cw_r, cb_r, wout, _row(ln3_g), _row(ln3_b), S=S, tm=tm)


def kernel(x, mem, positions, w_in_0, diff_lambda_0, diff_norm_0, gla_w2_0, gla_b2_0, gla_norm_0, w_mix_out_0, ln1_g_0, ln1_b_0, xa_wq_0, xa_wkv_0, xa_wo_0, ln2_g_0, ln2_b_0, ffn_w_in_0, ffn_conv_w_0, ffn_conv_b_0, ffn_w_out_0, ln3_g_0, ln3_b_0, w_in_1, gdn_conv_w_1, gdn_a_log_1, gdn_dt_bias_1, gdn_norm_1, w_mix_out_1, ln1_g_1, ln1_b_1, xa_wq_1, xa_wkv_1, xa_wo_1, ln2_g_1, ln2_b_1, ffn_w_in_1, ffn_conv_w_1, ffn_conv_b_1, ffn_w_out_1, ln3_g_1, ln3_b_1):
    B, S, _ = x.shape
    T = B * S
    tm = min(512, S)
    x2d = x.reshape(T, D_MODEL)
    mem2d = mem.reshape(B * N_MEM, D_MODEL)

    hp, tail = _proj0(x2d, positions.reshape(T, 1), _rope_consts(),
                      w_in_0[:, :P0_MAIN].astype(BF16), _tail_w(w_in_0, P0_MAIN), tm=tm)
    lam_init = 0.8 - 0.6 * math.exp(-0.3 * 0)
    oa = _diffattn(hp, diff_lambda_0.astype(F32), _row(diff_norm_0), B=B, S=S, tq=min(256, S), lam_init=lam_init)
    w2p = jnp.pad(gla_w2_0, ((0, LANES - GATE_RANK), (0, 0))).astype(BF16)
    ob = _gla(hp, tail, w2p, _row(gla_b2_0), _row(gla_norm_0), B=B, S=S, rb=min(512, S))
    h, hb = _outproj_ln([oa, ob], w_mix_out_0.astype(BF16), x2d, _row(ln1_g_0), _row(ln1_b_0), tm=tm)
    h, hb = _common(h, hb, mem2d, S,
                    (ln2_g_0, ln2_b_0, xa_wq_0, xa_wkv_0, xa_wo_0, ffn_w_in_0, ffn_conv_w_0, ffn_conv_b_0,
                     ffn_w_out_0, ln3_g_0, ln3_b_0), tm)

    hp, tail = _proj1(hb, w_in_1[:, :P1_MAIN].astype(BF16), _tail_w(w_in_1, P1_MAIN), tm=tm)
    head_par = (jnp.zeros((8, LANES), F32)
                .at[0, H_C:2 * H_C].set(gdn_a_log_1.astype(F32))
                .at[1, H_C:2 * H_C].set(gdn_dt_bias_1.astype(F32)))
    oc = _gdn(hp, tail, gdn_conv_w_1.astype(F32), head_par, _row(gdn_norm_1), B=B, S=S, rb=min(512, S))
    h, hb = _outproj_ln([oc], w_mix_out_1.astype(BF16), h, _row(ln1_g_1), _row(ln1_b_1), tm=tm)
    h, hb = _common(h, hb, mem2d, S,
                    (ln2_g_1, ln2_b_1, xa_wq_1, xa_wkv_1, xa_wo_1, ffn_w_in_1, ffn_conv_w_1, ffn_conv_b_1,
                     ffn_w_out_1, ln3_g_1, ln3_b_1), tm)
    return h.reshape(B, S, D_MODEL)
```

```python
import functools
import math

import jax
import jax.numpy as jnp
from jax import lax
from jax.experimental import pallas as pl
from jax.experimental.pallas import tpu as pltpu

F32 = jnp.float32
BF16 = jnp.bfloat16

D_MODEL = 1024
N_MEM = 256
H_A, D_A, DV_A = 4, 64, 128
ROPE_THETA, ROPE_DIM = 500000.0, 16
H_B, DK_B, DV_B = 4, 64, 128
GATE_RANK, GATE_NORMALIZER = 16, 16.0
H_C, DK_C, DV_C, CONV_C = 8, 128, 128, 4
CHUNK = 64
N_XA, D_XA = 4, 256
D_FF = 2816
FFN_CONV = 3
LN_EPS, RMS_EPS = 1e-5, 1e-6
DEPTH = 2
ALPHA = (2.0 * DEPTH) ** 0.25
P0_MAIN = 3072
P1_MAIN = 4096
LANES = 128
FF_CHUNK = 256
N_FF_CHUNKS = D_FF // FF_CHUNK
HALO = 16
VMEM_LIMIT = 56 * 1024 * 1024

NT_DIMS = (((1,), (1,)), ((), ()))
TN_DIMS = (((0,), (0,)), ((), ()))
HI = lax.Precision.HIGHEST


def _cparams(sem):
    return pltpu.CompilerParams(dimension_semantics=sem, vmem_limit_bytes=VMEM_LIMIT)


def _const_spec(shape):
    nd = len(shape)
    return pl.BlockSpec(shape, lambda *_: (0,) * nd, pipeline_mode=pl.Buffered(1))


def _dot(a, b):
    return jnp.dot(a, b, preferred_element_type=F32)


def _dot_nt(a, b, precision=None):
    return lax.dot_general(a, b, NT_DIMS, precision=precision, preferred_element_type=F32)


def _layer_norm(y, g, b):
    mu = jnp.mean(y, axis=-1, keepdims=True)
    yc = y - mu
    var = jnp.mean(yc * yc, axis=-1, keepdims=True)
    return yc * lax.rsqrt(var + LN_EPS) * g + b


def _sigmoid(x):
    return 1.0 / (1.0 + jnp.exp(-x))


def _softplus(x):
    return jnp.maximum(x, 0.0) + jnp.log(1.0 + jnp.exp(-jnp.abs(x)))


def _tril_mask(n, strict=False):
    r = lax.broadcasted_iota(jnp.int32, (n, n), 0)
    c = lax.broadcasted_iota(jnp.int32, (n, n), 1)
    return (c < r) if strict else (c <= r)


def _proj0_kernel(x_ref, pos_ref, rc_ref, w_ref, wt_ref, o_ref, t_ref):
    xb = x_ref[...].astype(BF16)
    ang = pos_ref[...].astype(F32) * rc_ref[0:1, :]
    c = jnp.cos(ang)
    s = jnp.sin(ang)
    s_lo = s * rc_ref[1:2, :]
    s_hi = s * rc_ref[2:3, :]
    q_scale = D_A ** -0.5
    for j in range(2):
        y = _dot(xb, w_ref[:, j * 512:(j + 1) * 512])
        for t in range(4):
            yt = y[:, t * LANES:(t + 1) * LANES]
            yt = yt * c + pltpu.roll(yt, LANES - 8, 1) * s_lo + pltpu.roll(yt, 8, 1) * s_hi
            if j == 0:
                yt = yt * q_scale
            col = j * 512 + t * LANES
            o_ref[:, col:col + LANES] = yt.astype(BF16)
    for col in range(1024, P0_MAIN, 512):
        o_ref[:, col:col + 512] = _dot(xb, w_ref[:, col:col + 512]).astype(BF16)
    t_ref[...] = _dot(xb, wt_ref[...])


def _proj0(x2d, pos2d, rope_c, w_main, w_tail, *, tm):
    T = x2d.shape[0]
    return pl.pallas_call(
        _proj0_kernel,
        grid=(T // tm,),
        in_specs=[
            pl.BlockSpec((tm, D_MODEL), lambda i: (i, 0)),
            pl.BlockSpec((tm, 1), lambda i: (i, 0)),
            _const_spec((8, LANES)),
            _const_spec((D_MODEL, P0_MAIN)),
            _const_spec((D_MODEL, LANES)),
        ],
        out_specs=[
            pl.BlockSpec((tm, P0_MAIN), lambda i: (i, 0)),
            pl.BlockSpec((tm, LANES), lambda i: (i, 0)),
        ],
        out_shape=[
            jax.ShapeDtypeStruct((T, P0_MAIN), BF16),
            jax.ShapeDtypeStruct((T, LANES), F32),
        ],
        compiler_params=_cparams(("parallel",)),
        name="proj0",
    )(x2d, pos2d, rope_c, w_main, w_tail)


def _proj1_kernel(x_ref, w_ref, wt_ref, o_ref, t_ref):
    xb = x_ref[...]
    for col in range(0, P1_MAIN, 512):
        o_ref[:, col:col + 512] = _dot(xb, w_ref[:, col:col + 512]).astype(BF16)
    t_ref[...] = _dot(xb, wt_ref[...])


def _proj1(xb2d, w_main, w_tail, *, tm):
    T = xb2d.shape[0]
    return pl.pallas_call(
        _proj1_kernel,
        grid=(T // tm,),
        in_specs=[
            pl.BlockSpec((tm, D_MODEL), lambda i: (i, 0)),
            _const_spec((D_MODEL, P1_MAIN)),
            _const_spec((D_MODEL, LANES)),
        ],
        out_specs=[
            pl.BlockSpec((tm, P1_MAIN), lambda i: (i, 0)),
            pl.BlockSpec((tm, LANES), lambda i: (i, 0)),
        ],
        out_shape=[
            jax.ShapeDtypeStruct((T, P1_MAIN), BF16),
            jax.ShapeDtypeStruct((T, LANES), F32),
        ],
        compiler_params=_cparams(("parallel",)),
        name="proj1",
    )(xb2d, w_main, w_tail)


def _kvproj_kernel(m_ref, w_ref, o_ref):
    mb = m_ref[...].astype(BF16)
    for col in range(0, 2 * D_MODEL, 512):
        o_ref[:, col:col + 512] = _dot(mb, w_ref[:, col:col + 512]).astype(BF16)


def _kvproj(mem2d, wkv, *, tm):
    R = mem2d.shape[0]
    return pl.pallas_call(
        _kvproj_kernel,
        grid=(R // tm,),
        in_specs=[
            pl.BlockSpec((tm, D_MODEL), lambda i: (i, 0)),
            _const_spec((D_MODEL, 2 * D_MODEL)),
        ],
        out_specs=pl.BlockSpec((tm, 2 * D_MODEL), lambda i: (i, 0)),
        out_shape=jax.ShapeDtypeStruct((R, 2 * D_MODEL), BF16),
        compiler_params=_cparams(("parallel",)),
        name="kvproj",
    )(mem2d, wkv)


def _diffattn_kernel(q_ref, k_ref, v_ref, lam_ref, g_ref, o_ref, *, S, tq, lam_init):
    lv = lam_ref[...]
    lam = (jnp.exp(jnp.sum(lv[0:1] * lv[1:2], axis=-1, keepdims=True))
           - jnp.exp(jnp.sum(lv[2:3] * lv[3:4], axis=-1, keepdims=True)) + lam_init)
    lane = lax.broadcasted_iota(jnp.int32, (tq, LANES), 1)
    r = lax.broadcasted_iota(jnp.int32, (2 * tq, tq), 0)
    c = lax.broadcasted_iota(jnp.int32, (2 * tq, tq), 1)
    causal = c <= jnp.where(r >= tq, r - tq, r)
    for qi in range(S // tq):
        n_keys = (qi + 1) * tq
        q = q_ref[qi * tq:(qi + 1) * tq, :]
        zero = jnp.zeros_like(q)
        qq = jnp.concatenate([jnp.where(lane < D_A, q, zero), jnp.where(lane >= D_A, q, zero)], axis=0)
        s = _dot_nt(qq, k_ref[0:n_keys, :])
        s_diag = jnp.where(causal, s[:, n_keys - tq:], -jnp.inf)
        s = s_diag if qi == 0 else jnp.concatenate([s[:, :n_keys - tq], s_diag], axis=1)
        p = jnp.exp(s - jnp.max(s, axis=-1, keepdims=True))
        l = jnp.sum(p, axis=-1, keepdims=True)
        o_all = _dot(p.astype(BF16), v_ref[0:n_keys, :]) / l
        o = o_all[:tq] - lam * o_all[tq:]
        ms = jnp.mean(o * o, axis=-1, keepdims=True)
        y = o * lax.rsqrt(ms + RMS_EPS) * g_ref[...] * (1.0 - lam_init)
        o_ref[qi * tq:(qi + 1) * tq, :] = y.astype(BF16)


def _diffattn(hp, lam_vecs, norm_g, *, B, S, tq, lam_init):
    T = B * S
    kern = functools.partial(_diffattn_kernel, S=S, tq=tq, lam_init=lam_init)
    return pl.pallas_call(
        kern,
        grid=(B, H_A),
        in_specs=[
            pl.BlockSpec((S, LANES), lambda b, h: (b, h)),
            pl.BlockSpec((S, LANES), lambda b, h: (b, H_A + h)),
            pl.BlockSpec((S, LANES), lambda b, h: (b, 2 * H_A + h)),
            pl.BlockSpec((4, D_A), lambda b, h: (0, 0)),
            pl.BlockSpec((1, DV_A), lambda b, h: (0, 0)),
        ],
        out_specs=pl.BlockSpec((S, LANES), lambda b, h: (b, h)),
        out_shape=jax.ShapeDtypeStruct((T, H_A * DV_A), BF16),
        compiler_params=_cparams(("parallel", "parallel")),
        name="diffattn",
    )(hp, hp, hp, lam_vecs, norm_g)


GLA_GROUP = 2 * CHUNK
GLA_LEVELS = (32, 16, 8, 4, 2, 1)


def _split_dot(m_bf16, hi, lo):
    return _dot(m_bf16, hi) + _dot(m_bf16, lo)


def _gla_kernel(q_ref, k_ref, v_ref, r_ref, t_ref, w2_ref, b2_ref, g_ref, o_ref, st_sc, *, rb):
    G = GLA_GROUP

    @pl.when(pl.program_id(1) == 0)
    def _():
        st_sc[...] = jnp.zeros(st_sc.shape, F32)

    r = lax.broadcasted_iota(jnp.int32, (G, G), 0)
    c = lax.broadcasted_iota(jnp.int32, (G, G), 1)
    tril2 = jnp.where(((r >= CHUNK) == (c >= CHUNK)) & (c <= r), 1.0, 0.0).astype(BF16)
    row = lax.broadcasted_iota(jnp.int32, (G, H_B * DK_B), 0)
    lane = lax.broadcasted_iota(jnp.int32, (G, LANES), 1)
    head_lanes = [lane < DK_B, lane >= DK_B]
    lane_c = lax.broadcasted_iota(jnp.int32, (CHUNK, LANES), 1)
    head_lanes_c = [lane_c < DK_B, lane_c >= DK_B]
    levels = []
    for w in GLA_LEVELS:
        same = (r // (2 * w)) == (c // (2 * w))
        r_in_i = (r % (2 * w)) >= w
        c_in_i = (c % (2 * w)) >= w
        in_range = (r_in_i & c_in_i & (c <= r)) | ((~r_in_i) & (~c_in_i) & (c > r))
        m_arg = jnp.where(same & in_range, 1.0, 0.0).astype(BF16)
        levels.append((m_arg, same, (row % (2 * w)) >= w))
    eye = r == c

    def group(gi, carry):
        r0 = pl.multiple_of(gi * G, G)
        rows = pl.ds(r0, G)
        x = _dot(t_ref[rows, :].astype(BF16), w2_ref[...]) + b2_ref[...]
        gk = (jnp.minimum(x, 0.0) - jnp.log(1.0 + jnp.exp(-jnp.abs(x)))) * (1.0 / GATE_NORMALIZER)
        hi = gk.astype(BF16)
        lo = (gk - hi.astype(F32)).astype(BF16)
        b = _split_dot(tril2, hi, lo)
        q = q_ref[rows, :].astype(F32) * (DK_B ** -0.5)
        k = k_ref[rows, :].astype(F32)
        v = v_ref[rows, :]
        q_lv, k_lv, masks = [], [], []
        for m_arg, same, row_in_i in levels:
            e = jnp.exp(_split_dot(m_arg, hi, lo))
            q_lv.append(jnp.where(row_in_i, q * e, 0.0))
            k_lv.append(jnp.where(row_in_i, 0.0, k * e).astype(BF16))
            masks.append(same)
        q_lv.append(q)
        k_lv.append(k.astype(BF16))
        masks.append(eye)
        atts = [None] * H_B
        for ql, kl_, mask in zip(q_lv, k_lv, masks):
            for h in range(H_B):
                blk = slice((h // 2) * LANES, (h // 2 + 1) * LANES)
                qm = jnp.where(head_lanes[h % 2], ql[:, blk], 0.0).astype(BF16)
                part = jnp.where(mask, _dot_nt(qm, kl_[:, blk]), 0.0)
                atts[h] = part if atts[h] is None else atts[h] + part
        avs = [_dot(atts[h].astype(BF16), v[:, h * DV_B:(h + 1) * DV_B]) for h in range(H_B)]
        qe = q * jnp.exp(b)
        sts = [st_sc[p] for p in range(H_B // 2)]
        inters = [[] for _ in range(H_B)]
        for c2 in range(G // CHUNK):
            cs = slice(c2 * CHUNK, (c2 + 1) * CHUNK)
            for h in range(H_B):
                blk = slice((h // 2) * LANES, (h // 2 + 1) * LANES)
                qm = jnp.where(head_lanes_c[h % 2], qe[cs, blk], 0.0).astype(BF16)
                inters[h].append(_dot_nt(qm, sts[h // 2].astype(BF16)))
            bl = b[(c2 + 1) * CHUNK - 1:(c2 + 1) * CHUNK, :]
            kl = (k[cs] * jnp.exp(bl - b[cs])).astype(BF16)
            ebl = jnp.exp(bl)
            kvs = [lax.dot_general(v[cs, h * DV_B:(h + 1) * DV_B], kl[:, (h // 2) * LANES:(h // 2 + 1) * LANES],
                                   TN_DIMS, preferred_element_type=F32) for h in range(H_B)]
            sts = [sts[p] * ebl[:, p * LANES:(p + 1) * LANES]
                   + jnp.where(head_lanes[0], kvs[2 * p], kvs[2 * p + 1]) for p in range(H_B // 2)]
        for p in range(H_B // 2):
            st_sc[p] = sts[p]
        for h in range(H_B):
            vs = slice(h * DV_B, (h + 1) * DV_B)
            o = jnp.concatenate(inters[h], axis=0) + avs[h]
            ms = jnp.mean(o * o, axis=-1, keepdims=True)
            gate = r_ref[rows, vs].astype(F32)
            y = o * lax.rsqrt(ms + RMS_EPS) * g_ref[...] * (gate * _sigmoid(gate))
            o_ref[rows, vs] = y.astype(BF16)
        return carry

    lax.fori_loop(0, rb // G, group, 0)


def _gla(hp, tail, w2p, b2, norm_g, *, B, S, rb):
    nb = S // rb
    T = B * S
    kern = functools.partial(_gla_kernel, rb=rb)
    return pl.pallas_call(
        kern,
        grid=(B, nb),
        in_specs=[
            pl.BlockSpec((rb, H_B * DK_B), lambda b, i: (b * nb + i, 6)),
            pl.BlockSpec((rb, H_B * DK_B), lambda b, i: (b * nb + i, 7)),
            pl.BlockSpec((rb, H_B * DV_B), lambda b, i: (b * nb + i, 4)),
            pl.BlockSpec((rb, H_B * DV_B), lambda b, i: (b * nb + i, 5)),
            pl.BlockSpec((rb, LANES), lambda b, i: (b * nb + i, 0)),
            pl.BlockSpec((LANES, H_B * DK_B), lambda b, i: (0, 0)),
            pl.BlockSpec((1, H_B * DK_B), lambda b, i: (0, 0)),
            pl.BlockSpec((1, DV_B), lambda b, i: (0, 0)),
        ],
        out_specs=pl.BlockSpec((rb, H_B * DV_B), lambda b, i: (b * nb + i, 0)),
        out_shape=jax.ShapeDtypeStruct((T, H_B * DV_B), BF16),
        scratch_shapes=[pltpu.VMEM((H_B // 2, DV_B, 2 * DK_B), F32)],
        compiler_params=_cparams(("parallel", "arbitrary")),
        name="gla",
    )(hp, hp, hp, hp, tail, w2p, b2, norm_g)


GDN_GROUP = 2 * CHUNK


def _gdn_kernel(q_ref, k_ref, v_ref, z_ref, t_ref, cw_ref, hp_ref, g_ref, o_ref, s_sc, xwin_sc, *, rb):
    G = GDN_GROUP

    @pl.when(pl.program_id(1) == 0)
    def _():
        s_sc[...] = jnp.zeros(s_sc.shape, F32)
        xwin_sc[:, 0:8, :] = jnp.zeros((3, 8, H_C * DK_C), F32)

    r = lax.broadcasted_iota(jnp.int32, (G, G), 0)
    c = lax.broadcasted_iota(jnp.int32, (G, G), 1)
    same = (r >= CHUNK) == (c >= CHUNK)
    incl = same & (c <= r)
    strict = same & (c < r)
    tril2 = incl.astype(F32)
    eye = (r == c).astype(F32)

    def conv_silu(idx, h):
        ls = slice(h * LANES, (h + 1) * LANES)
        w = cw_ref[:, idx * H_C * LANES + h * LANES: idx * H_C * LANES + (h + 1) * LANES]
        y = w[CONV_C - 1:CONV_C, :] * xwin_sc[idx, 8:8 + G, ls]
        for j in range(CONV_C - 1):
            y = y + w[j:j + 1, :] * xwin_sc[idx, 5 + j:5 + j + G, ls]
        return y * _sigmoid(y)

    def group(gi, carry):
        r0 = pl.multiple_of(gi * G, G)
        xwin_sc[0, 8:8 + G, :] = q_ref[pl.ds(r0, G), :].astype(F32)
        xwin_sc[1, 8:8 + G, :] = k_ref[pl.ds(r0, G), :].astype(F32)
        xwin_sc[2, 8:8 + G, :] = v_ref[pl.ds(r0, G), :].astype(F32)
        t = t_ref[pl.ds(r0, G), :]
        beta_all = _sigmoid(t)
        g_all = -jnp.exp(hp_ref[0:1, :]) * _softplus(t + hp_ref[1:2, :])
        cum_all = jnp.dot(tril2, g_all, precision=HI, preferred_element_type=F32)
        cum_all_t = cum_all.T
        heads = range(H_C)
        qs = [conv_silu(0, h) for h in heads]
        ks = [conv_silu(1, h) for h in heads]
        vs = [conv_silu(2, h) for h in heads]
        qs = [q * lax.rsqrt(jnp.sum(q * q, axis=-1, keepdims=True) + RMS_EPS) * (DK_C ** -0.5) for q in qs]
        ks = [k * lax.rsqrt(jnp.sum(k * k, axis=-1, keepdims=True) + RMS_EPS) for k in ks]
        bhs = [jnp.broadcast_to(beta_all[:, h:h + 1], (G, LANES)) for h in heads]
        chs = [jnp.broadcast_to(cum_all[:, H_C + h:H_C + h + 1], (G, LANES)) for h in heads]
        crows = [jnp.broadcast_to(cum_all_t[H_C + h:H_C + h + 1, :], (G, G)) for h in heads]
        decs = [jnp.where(incl, jnp.exp(ch - crow), 0.0) for ch, crow in zip(chs, crows)]
        egs = [jnp.exp(ch) for ch in chs]
        kbs = [k * bh for k, bh in zip(ks, bhs)]
        khbs = [k.astype(BF16) for k in ks]
        kks = [_dot_nt(kb.astype(BF16), khb) for kb, khb in zip(kbs, khbs)]
        ps = [-jnp.where(strict, kk * dec, 0.0) for kk, dec in zip(kks, decs)]
        tinvs = [eye + p for p in ps]
        for _ in range(5):
            pbs = [p.astype(BF16) for p in ps]
            ps = [_dot(pb, pb) for pb in pbs]
            tinvs = [tinv + _dot(p.astype(BF16), tinv.astype(BF16)) for p, tinv in zip(ps, tinvs)]
        rhss = [jnp.concatenate([v * bh, kb * eg], axis=1).astype(BF16)
                for v, bh, kb, eg in zip(vs, bhs, kbs, egs)]
        uws = [_dot(tinv.astype(BF16), rhs) for tinv, rhs in zip(tinvs, rhss)]
        qks = [_dot_nt(q.astype(BF16), khb) * dec for q, khb, dec in zip(qs, khbs, decs)]
        qgs = [q * eg for q, eg in zip(qs, egs)]
        sts = [s_sc[h] for h in heads]
        outs = [[] for _ in heads]
        for c2 in range(G // CHUNK):
            cs = slice(c2 * CHUNK, (c2 + 1) * CHUNK)
            wss = [_dot(jnp.concatenate([uw[cs, DV_C:], qg[cs]], axis=0).astype(BF16), st.astype(BF16))
                   for uw, qg, st in zip(uws, qgs, sts)]
            v_news = [uw[cs, :DV_C] - ws[:CHUNK] for uw, ws in zip(uws, wss)]
            for h in heads:
                outs[h].append(wss[h][CHUNK:] + _dot(qks[h][cs, cs].astype(BF16), v_news[h].astype(BF16)))
            gls = [ch[(c2 + 1) * CHUNK - 1:(c2 + 1) * CHUNK, :] for ch in chs]
            kgs = [k[cs] * jnp.exp(gl - ch[cs]) for k, gl, ch in zip(ks, gls, chs)]
            sts = [st * jnp.exp(gl) + _dot(kg.T.astype(BF16), v_new.astype(BF16))
                   for st, gl, kg, v_new in zip(sts, gls, kgs, v_news)]
        for h in heads:
            ls = slice(h * LANES, (h + 1) * LANES)
            s_sc[h] = sts[h]
            o = jnp.concatenate(outs[h], axis=0)
            ms = jnp.mean(o * o, axis=-1, keepdims=True)
            z = z_ref[pl.ds(r0, G), ls].astype(F32)
            y = o * lax.rsqrt(ms + RMS_EPS) * g_ref[...] * (z * _sigmoid(z))
            o_ref[pl.ds(r0, G), ls] = y.astype(BF16)
        xwin_sc[:, 0:8, :] = xwin_sc[:, G:G + 8, :]
        return carry

    lax.fori_loop(0, rb // G, group, 0)


def _gdn(hp, tail, conv_w, head_par, norm_g, *, B, S, rb):
    nb = S // rb
    T = B * S
    width = H_C * DK_C
    kern = functools.partial(_gdn_kernel, rb=rb)
    return pl.pallas_call(
        kern,
        grid=(B, nb),
        in_specs=[
            pl.BlockSpec((rb, width), lambda b, i: (b * nb + i, 0)),
            pl.BlockSpec((rb, width), lambda b, i: (b * nb + i, 1)),
            pl.BlockSpec((rb, width), lambda b, i: (b * nb + i, 2)),
            pl.BlockSpec((rb, width), lambda b, i: (b * nb + i, 3)),
            pl.BlockSpec((rb, LANES), lambda b, i: (b * nb + i, 0)),
            pl.BlockSpec((CONV_C, 3 * width), lambda b, i: (0, 0)),
            pl.BlockSpec((8, LANES), lambda b, i: (0, 0)),
            pl.BlockSpec((1, DV_C), lambda b, i: (0, 0)),
        ],
        out_specs=pl.BlockSpec((rb, width), lambda b, i: (b * nb + i, 0)),
        out_shape=jax.ShapeDtypeStruct((T, width), BF16),
        scratch_shapes=[
            pltpu.VMEM((H_C, DK_C, DV_C), F32),
            pltpu.VMEM((3, GDN_GROUP + 8, width), F32),
        ],
        compiler_params=_cparams(("parallel", "arbitrary")),
        name="gdn",
    )(hp, hp, hp, hp, tail, conv_w, head_par, norm_g)


N_POST_PARAMS = 15


def _post_kernel(*refs, n_in, tm, tiles_per_seq, emit_bf16):
    ins = refs[:n_in]
    (wmix_ref, res_ref, g1_ref, b1_ref, kv_ref, wq_ref, wo_ref, g2_ref, b2_ref,
     win_ref, cw_ref, cb_ref, wout_ref, g3_ref, b3_ref) = refs[n_in:n_in + N_POST_PARAMS]
    n_out = 2 if emit_bf16 else 1
    outs = refs[n_in + N_POST_PARAMS:n_in + N_POST_PARAMS + n_out]
    att_sc, xcat_sc, hid_sc, act_sc = refs[n_in + N_POST_PARAMS + n_out:]

    y = ALPHA * res_ref[...]
    off = 0
    for r in ins:
        kdim = r.shape[1]
        y = y + _dot(r[...], wmix_ref[off:off + kdim, :])
        off += kdim
    h1 = _layer_norm(y, g1_ref[...], b1_ref[...])

    q = (_dot(h1.astype(BF16), wq_ref[...]) * (D_XA ** -0.5)).astype(BF16)
    for hd in range(N_XA):
        cs = slice(hd * D_XA, (hd + 1) * D_XA)
        s = _dot_nt(q[:, cs], kv_ref[:, cs])
        p = jnp.exp(s - jnp.max(s, axis=-1, keepdims=True))
        l = jnp.sum(p, axis=-1, keepdims=True)
        vs = slice(D_MODEL + hd * D_XA, D_MODEL + (hd + 1) * D_XA)
        att_sc[:, cs] = (_dot(p.astype(BF16), kv_ref[:, vs]) / l).astype(BF16)
    h2 = _layer_norm(ALPHA * h1 + _dot(att_sc[...], wo_ref[...]), g2_ref[...], b2_ref[...])

    is_start = (pl.program_id(0) % tiles_per_seq) == 0

    @pl.when(is_start)
    def _():
        xcat_sc[0:HALO, :] = jnp.zeros((HALO, D_MODEL), BF16)

    @pl.when(jnp.logical_not(is_start))
    def _():
        xcat_sc[0:HALO, :] = xcat_sc[tm:tm + HALO, :]

    xcat_sc[HALO:HALO + tm, :] = h2.astype(BF16)
    xc = xcat_sc[...]

    def conv(col):
        hid_sc[...] = _dot(xc, win_ref[:, col:col + FF_CHUNK])
        w = cw_ref[:, col:col + FF_CHUNK]
        y = cb_ref[:, col:col + FF_CHUNK] + w[FFN_CONV - 1:FFN_CONV, :] * hid_sc[HALO:HALO + tm, :]
        for j in range(FFN_CONV - 1):
            o = HALO - (FFN_CONV - 1) + j
            y = y + w[j:j + 1, :] * hid_sc[o:o + tm, :]
        return y

    for c in range(N_FF_CHUNKS):
        gate = conv(c * FF_CHUNK)
        up = conv(D_FF + c * FF_CHUNK)
        act_sc[:, c * FF_CHUNK:(c + 1) * FF_CHUNK] = (gate * _sigmoid(gate) * up).astype(BF16)
    out = _layer_norm(ALPHA * h2 + _dot(act_sc[...], wout_ref[...]), g3_ref[...], b3_ref[...])
    outs[0][...] = out
    if emit_bf16:
        outs[1][...] = out.astype(BF16)


def _post(ins, res, kv, params, *, S, tm, emit_bf16):
    (w_mix, g1, b1, wq, wo, g2, b2, w_in, cw, cb, w_out, g3, b3) = params
    T = res.shape[0]
    tiles_per_seq = S // tm
    kern = functools.partial(_post_kernel, n_in=len(ins), tm=tm, tiles_per_seq=tiles_per_seq, emit_bf16=emit_bf16)
    row_tile = pl.BlockSpec((tm, D_MODEL), lambda i: (i, 0))
    vec = pl.BlockSpec((1, D_MODEL), lambda i: (0, 0))
    in_specs = [pl.BlockSpec((tm, a.shape[1]), lambda i: (i, 0)) for a in ins]
    in_specs += [
        _const_spec(w_mix.shape), row_tile, vec, vec,
        pl.BlockSpec((N_MEM, 2 * D_MODEL), lambda i: (i // tiles_per_seq, 0)),
        _const_spec(wq.shape), _const_spec(wo.shape), vec, vec,
        _const_spec(w_in.shape), _const_spec(cw.shape), _const_spec(cb.shape), _const_spec(w_out.shape), vec, vec,
    ]
    out_specs = [row_tile]
    out_shape = [jax.ShapeDtypeStruct((T, D_MODEL), F32)]
    if emit_bf16:
        out_specs.append(row_tile)
        out_shape.append(jax.ShapeDtypeStruct((T, D_MODEL), BF16))
    return pl.pallas_call(
        kern,
        grid=(T // tm,),
        in_specs=in_specs,
        out_specs=out_specs,
        out_shape=out_shape,
        scratch_shapes=[
            pltpu.VMEM((tm, D_MODEL), BF16),
            pltpu.VMEM((tm + HALO, D_MODEL), BF16),
            pltpu.VMEM((tm + HALO, FF_CHUNK), F32),
            pltpu.VMEM((tm, D_FF), BF16),
        ],
        compiler_params=_cparams(("arbitrary",)),
        name="post",
    )(*ins, w_mix, res, g1, b1, kv, wq, wo, g2, b2, w_in, cw, cb, w_out, g3, b3)


def _rope_consts():
    inv_freq = ROPE_THETA ** (-jnp.arange(0, ROPE_DIM, 2, dtype=F32) / ROPE_DIM)
    d = jnp.arange(LANES) % D_A
    half = ROPE_DIM // 2
    freq = jnp.where(d < ROPE_DIM, inv_freq[d % half], 0.0)
    lo = jnp.where(d < half, -1.0, 0.0)
    hi = jnp.where((d >= half) & (d < ROPE_DIM), 1.0, 0.0)
    return jnp.zeros((8, LANES), F32).at[0].set(freq).at[1].set(lo).at[2].set(hi)


def _row(v):
    return v.reshape(1, -1).astype(F32)


def _tail_w(w, n_main):
    tail = w[:, n_main:]
    return jnp.pad(tail, ((0, 0), (0, LANES - tail.shape[1]))).astype(BF16)


def _post_layer(ins, res, mem2d, S, params, tm, emit_bf16):
    (w_mix, ln1_g, ln1_b, xa_wq, xa_wkv, xa_wo, ln2_g, ln2_b, f_in, f_cw, f_cb, f_out, ln3_g, ln3_b) = params
    kv = _kvproj(mem2d, xa_wkv.astype(BF16), tm=tm)
    prepared = (w_mix.astype(BF16), _row(ln1_g), _row(ln1_b), xa_wq.astype(BF16), xa_wo.astype(BF16),
                _row(ln2_g), _row(ln2_b), f_in.astype(BF16), f_cw.astype(F32), _row(f_cb), f_out.astype(BF16),
                _row(ln3_g), _row(ln3_b))
    return _post(ins, res, kv, prepared, S=S, tm=tm, emit_bf16=emit_bf16)


def kernel(x, mem, positions, w_in_0, diff_lambda_0, diff_norm_0, gla_w2_0, gla_b2_0, gla_norm_0, w_mix_out_0, ln1_g_0, ln1_b_0, xa_wq_0, xa_wkv_0, xa_wo_0, ln2_g_0, ln2_b_0, ffn_w_in_0, ffn_conv_w_0, ffn_conv_b_0, ffn_w_out_0, ln3_g_0, ln3_b_0, w_in_1, gdn_conv_w_1, gdn_a_log_1, gdn_dt_bias_1, gdn_norm_1, w_mix_out_1, ln1_g_1, ln1_b_1, xa_wq_1, xa_wkv_1, xa_wo_1, ln2_g_1, ln2_b_1, ffn_w_in_1, ffn_conv_w_1, ffn_conv_b_1, ffn_w_out_1, ln3_g_1, ln3_b_1):
    B, S, _ = x.shape
    T = B * S
    tm = min(512, S)
    x2d = x.reshape(T, D_MODEL)
    mem2d = mem.reshape(B * N_MEM, D_MODEL)

    hp, tail = _proj0(x2d, positions.reshape(T, 1), _rope_consts(),
                      w_in_0[:, :P0_MAIN].astype(BF16), _tail_w(w_in_0, P0_MAIN), tm=tm)
    lam_init = 0.8 - 0.6 * math.exp(-0.3 * 0)
    oa = _diffattn(hp, diff_lambda_0.astype(F32), _row(diff_norm_0), B=B, S=S, tq=min(256, S), lam_init=lam_init)
    w2p = jnp.pad(gla_w2_0, ((0, LANES - GATE_RANK), (0, 0))).astype(BF16)
    ob = _gla(hp, tail, w2p, _row(gla_b2_0), _row(gla_norm_0), B=B, S=S, rb=min(512, S))
    h, hb = _post_layer([oa, ob], x2d, mem2d, S,
                        (w_mix_out_0, ln1_g_0, ln1_b_0, xa_wq_0, xa_wkv_0, xa_wo_0, ln2_g_0, ln2_b_0,
                         ffn_w_in_0, ffn_conv_w_0, ffn_conv_b_0, ffn_w_out_0, ln3_g_0, ln3_b_0), tm, True)

    hp, tail = _proj1(hb, w_in_1[:, :P1_MAIN].astype(BF16), _tail_w(w_in_1, P1_MAIN), tm=tm)
    head_par = (jnp.zeros((8, LANES), F32)
                .at[0, H_C:2 * H_C].set(gdn_a_log_1.astype(F32))
                .at[1, H_C:2 * H_C].set(gdn_dt_bias_1.astype(F32)))
    oc = _gdn(hp, tail, gdn_conv_w_1.astype(F32), head_par, _row(gdn_norm_1), B=B, S=S, rb=min(512, S))
    (out,) = _post_layer([oc], h, mem2d, S,
                         (w_mix_out_1, ln1_g_1, ln1_b_1, xa_wq_1, xa_wkv_1, xa_wo_1, ln2_g_1, ln2_b_1,
                          ffn_w_in_1, ffn_conv_w_1, ffn_conv_b_1, ffn_w_out_1, ln3_g_1, ln3_b_1), tm, False)
    return out.reshape(B, S, D_MODEL)
```

```python
import functools
import math

import jax
import jax.numpy as jnp
from jax import lax
from jax.experimental import pallas as pl
from jax.experimental.pallas import tpu as pltpu

F32 = jnp.float32
BF16 = jnp.bfloat16

D_MODEL = 1024
N_MEM = 256
H_A, D_A, DV_A = 4, 64, 128
ROPE_THETA, ROPE_DIM = 500000.0, 16
H_B, DK_B, DV_B = 4, 64, 128
GATE_RANK, GATE_NORMALIZER = 16, 16.0
H_C, DK_C, DV_C, CONV_C = 8, 128, 128, 4
CHUNK = 64
N_XA, D_XA = 4, 256
D_FF = 2816
FFN_CONV = 3
LN_EPS, RMS_EPS = 1e-5, 1e-6
DEPTH = 2
ALPHA = (2.0 * DEPTH) ** 0.25
P0_MAIN = 3072
P1_MAIN = 4096
LANES = 128
FF_CHUNK = 256
N_FF_CHUNKS = D_FF // FF_CHUNK
HALO = 16
VMEM_LIMIT = 56 * 1024 * 1024

NT_DIMS = (((1,), (1,)), ((), ()))
TN_DIMS = (((0,), (0,)), ((), ()))
HI = lax.Precision.HIGHEST


def _cparams(sem):
    return pltpu.CompilerParams(dimension_semantics=sem, vmem_limit_bytes=VMEM_LIMIT)


def _const_spec(shape):
    nd = len(shape)
    return pl.BlockSpec(shape, lambda *_: (0,) * nd, pipeline_mode=pl.Buffered(1))


def _dot(a, b):
    return jnp.dot(a, b, preferred_element_type=F32)


def _dot_nt(a, b, precision=None):
    return lax.dot_general(a, b, NT_DIMS, precision=precision, preferred_element_type=F32)


def _layer_norm(y, g, b):
    mu = jnp.mean(y, axis=-1, keepdims=True)
    yc = y - mu
    var = jnp.mean(yc * yc, axis=-1, keepdims=True)
    return yc * lax.rsqrt(var + LN_EPS) * g + b


def _sigmoid(x):
    return 1.0 / (1.0 + jnp.exp(-x))


def _softplus(x):
    return jnp.maximum(x, 0.0) + jnp.log(1.0 + jnp.exp(-jnp.abs(x)))


def _tril_mask(n, strict=False):
    r = lax.broadcasted_iota(jnp.int32, (n, n), 0)
    c = lax.broadcasted_iota(jnp.int32, (n, n), 1)
    return (c < r) if strict else (c <= r)


def _proj0_kernel(x_ref, pos_ref, rc_ref, w_ref, wt_ref, o_ref, t_ref):
    xb = x_ref[...].astype(BF16)
    ang = pos_ref[...].astype(F32) * rc_ref[0:1, :]
    c = jnp.cos(ang)
    s = jnp.sin(ang)
    s_lo = s * rc_ref[1:2, :]
    s_hi = s * rc_ref[2:3, :]
    q_scale = D_A ** -0.5
    for j in range(2):
        y = _dot(xb, w_ref[:, j * 512:(j + 1) * 512])
        for t in range(4):
            yt = y[:, t * LANES:(t + 1) * LANES]
            yt = yt * c + pltpu.roll(yt, LANES - 8, 1) * s_lo + pltpu.roll(yt, 8, 1) * s_hi
            if j == 0:
                yt = yt * q_scale
            col = j * 512 + t * LANES
            o_ref[:, col:col + LANES] = yt.astype(BF16)
    for col in range(1024, P0_MAIN, 512):
        o_ref[:, col:col + 512] = _dot(xb, w_ref[:, col:col + 512]).astype(BF16)
    t_ref[...] = _dot(xb, wt_ref[...])


def _proj0(x2d, pos2d, rope_c, w_main, w_tail, *, tm):
    T = x2d.shape[0]
    return pl.pallas_call(
        _proj0_kernel,
        grid=(T // tm,),
        in_specs=[
            pl.BlockSpec((tm, D_MODEL), lambda i: (i, 0)),
            pl.BlockSpec((tm, 1), lambda i: (i, 0)),
            _const_spec((8, LANES)),
            _const_spec((D_MODEL, P0_MAIN)),
            _const_spec((D_MODEL, LANES)),
        ],
        out_specs=[
            pl.BlockSpec((tm, P0_MAIN), lambda i: (i, 0)),
            pl.BlockSpec((tm, LANES), lambda i: (i, 0)),
        ],
        out_shape=[
            jax.ShapeDtypeStruct((T, P0_MAIN), BF16),
            jax.ShapeDtypeStruct((T, LANES), F32),
        ],
        compiler_params=_cparams(("parallel",)),
        name="proj0",
    )(x2d, pos2d, rope_c, w_main, w_tail)


QKV_C = 3 * H_C * DK_C
PROJ_CHUNK = 512


def _proj1_kernel(x_ref, w_ref, wt_ref, cw_ref, o_ref, t_ref, ybuf_sc, carry_sc, *, tm, tiles_per_seq):
    xb = x_ref[...]

    @pl.when((pl.program_id(0) % tiles_per_seq) == 0)
    def _():
        carry_sc[...] = jnp.zeros(carry_sc.shape, F32)

    for ci, col in enumerate(range(0, QKV_C, PROJ_CHUNK)):
        ybuf_sc[0:8, :] = carry_sc[ci]
        ybuf_sc[8:8 + tm, :] = _dot(xb, w_ref[:, col:col + PROJ_CHUNK])
        carry_sc[ci] = ybuf_sc[tm:tm + 8, :]
        w = cw_ref[:, col:col + PROJ_CHUNK]
        y = w[CONV_C - 1:CONV_C, :] * ybuf_sc[8:8 + tm, :]
        for j in range(CONV_C - 1):
            y = y + w[j:j + 1, :] * ybuf_sc[5 + j:5 + j + tm, :]
        y = y * _sigmoid(y)
        if col < 2 * H_C * DK_C:
            scale = DK_C ** -0.5 if col < H_C * DK_C else 1.0
            for t in range(PROJ_CHUNK // LANES):
                yt = y[:, t * LANES:(t + 1) * LANES]
                yt = yt * (lax.rsqrt(jnp.sum(yt * yt, axis=-1, keepdims=True) + RMS_EPS) * scale)
                o_ref[:, col + t * LANES:col + (t + 1) * LANES] = yt.astype(BF16)
        else:
            o_ref[:, col:col + PROJ_CHUNK] = y.astype(BF16)
    for col in range(QKV_C, P1_MAIN, PROJ_CHUNK):
        o_ref[:, col:col + PROJ_CHUNK] = _dot(xb, w_ref[:, col:col + PROJ_CHUNK]).astype(BF16)
    t_ref[...] = _dot(xb, wt_ref[...])


def _proj1(xb2d, w_main, w_tail, conv_w, *, S, tm):
    T = xb2d.shape[0]
    kern = functools.partial(_proj1_kernel, tm=tm, tiles_per_seq=S // tm)
    return pl.pallas_call(
        kern,
        grid=(T // tm,),
        in_specs=[
            pl.BlockSpec((tm, D_MODEL), lambda i: (i, 0)),
            _const_spec((D_MODEL, P1_MAIN)),
            _const_spec((D_MODEL, LANES)),
            _const_spec((CONV_C, QKV_C)),
        ],
        out_specs=[
            pl.BlockSpec((tm, P1_MAIN), lambda i: (i, 0)),
            pl.BlockSpec((tm, LANES), lambda i: (i, 0)),
        ],
        out_shape=[
            jax.ShapeDtypeStruct((T, P1_MAIN), BF16),
            jax.ShapeDtypeStruct((T, LANES), F32),
        ],
        scratch_shapes=[
            pltpu.VMEM((tm + 8, PROJ_CHUNK), F32),
            pltpu.VMEM((QKV_C // PROJ_CHUNK, 8, PROJ_CHUNK), F32),
        ],
        compiler_params=_cparams(("arbitrary",)),
        name="proj1",
    )(xb2d, w_main, w_tail, conv_w)


def _kvproj_kernel(m_ref, w_ref, o_ref):
    mb = m_ref[...].astype(BF16)
    for col in range(0, 2 * D_MODEL, 512):
        o_ref[:, col:col + 512] = _dot(mb, w_ref[:, col:col + 512]).astype(BF16)


def _kvproj(mem2d, wkv, *, tm):
    R = mem2d.shape[0]
    return pl.pallas_call(
        _kvproj_kernel,
        grid=(R // tm,),
        in_specs=[
            pl.BlockSpec((tm, D_MODEL), lambda i: (i, 0)),
            _const_spec((D_MODEL, 2 * D_MODEL)),
        ],
        out_specs=pl.BlockSpec((tm, 2 * D_MODEL), lambda i: (i, 0)),
        out_shape=jax.ShapeDtypeStruct((R, 2 * D_MODEL), BF16),
        compiler_params=_cparams(("parallel",)),
        name="kvproj",
    )(mem2d, wkv)


def _diffattn_kernel(q_ref, k_ref, v_ref, lam_ref, g_ref, o_ref, *, S, tq, lam_init):
    lv = lam_ref[...]
    lam = (jnp.exp(jnp.sum(lv[0:1] * lv[1:2], axis=-1, keepdims=True))
           - jnp.exp(jnp.sum(lv[2:3] * lv[3:4], axis=-1, keepdims=True)) + lam_init)
    lane = lax.broadcasted_iota(jnp.int32, (tq, LANES), 1)
    r = lax.broadcasted_iota(jnp.int32, (2 * tq, tq), 0)
    c = lax.broadcasted_iota(jnp.int32, (2 * tq, tq), 1)
    causal = c <= jnp.where(r >= tq, r - tq, r)
    for qi in range(S // tq):
        n_keys = (qi + 1) * tq
        q = q_ref[qi * tq:(qi + 1) * tq, :]
        zero = jnp.zeros_like(q)
        qq = jnp.concatenate([jnp.where(lane < D_A, q, zero), jnp.where(lane >= D_A, q, zero)], axis=0)
        s = _dot_nt(qq, k_ref[0:n_keys, :])
        s_diag = jnp.where(causal, s[:, n_keys - tq:], -jnp.inf)
        s = s_diag if qi == 0 else jnp.concatenate([s[:, :n_keys - tq], s_diag], axis=1)
        p = jnp.exp(s - jnp.max(s, axis=-1, keepdims=True))
        l = jnp.sum(p, axis=-1, keepdims=True)
        o_all = _dot(p.astype(BF16), v_ref[0:n_keys, :]) / l
        o = o_all[:tq] - lam * o_all[tq:]
        ms = jnp.mean(o * o, axis=-1, keepdims=True)
        y = o * lax.rsqrt(ms + RMS_EPS) * g_ref[...] * (1.0 - lam_init)
        o_ref[qi * tq:(qi + 1) * tq, :] = y.astype(BF16)


def _diffattn(hp, lam_vecs, norm_g, *, B, S, tq, lam_init):
    T = B * S
    kern = functools.partial(_diffattn_kernel, S=S, tq=tq, lam_init=lam_init)
    return pl.pallas_call(
        kern,
        grid=(B, H_A),
        in_specs=[
            pl.BlockSpec((S, LANES), lambda b, h: (b, h)),
            pl.BlockSpec((S, LANES), lambda b, h: (b, H_A + h)),
            pl.BlockSpec((S, LANES), lambda b, h: (b, 2 * H_A + h)),
            pl.BlockSpec((4, D_A), lambda b, h: (0, 0)),
            pl.BlockSpec((1, DV_A), lambda b, h: (0, 0)),
        ],
        out_specs=pl.BlockSpec((S, LANES), lambda b, h: (b, h)),
        out_shape=jax.ShapeDtypeStruct((T, H_A * DV_A), BF16),
        compiler_params=_cparams(("parallel", "parallel")),
        name="diffattn",
    )(hp, hp, hp, lam_vecs, norm_g)


GLA_GROUP = 2 * CHUNK
GLA_LEVELS = (32, 16, 8, 4, 2, 1)


def _split_dot(m_bf16, hi, lo):
    return _dot(m_bf16, hi) + _dot(m_bf16, lo)


def _gla_kernel(q_ref, k_ref, v_ref, r_ref, t_ref, w2_ref, b2_ref, g_ref, o_ref, st_sc, *, rb):
    G = GLA_GROUP

    @pl.when(pl.program_id(1) == 0)
    def _():
        st_sc[...] = jnp.zeros(st_sc.shape, F32)

    r = lax.broadcasted_iota(jnp.int32, (G, G), 0)
    c = lax.broadcasted_iota(jnp.int32, (G, G), 1)
    tril2 = jnp.where(((r >= CHUNK) == (c >= CHUNK)) & (c <= r), 1.0, 0.0).astype(BF16)
    row = lax.broadcasted_iota(jnp.int32, (G, H_B * DK_B), 0)
    lane = lax.broadcasted_iota(jnp.int32, (G, LANES), 1)
    head_lanes = [lane < DK_B, lane >= DK_B]
    lane_c = lax.broadcasted_iota(jnp.int32, (CHUNK, LANES), 1)
    head_lanes_c = [lane_c < DK_B, lane_c >= DK_B]
    levels = []
    for w in GLA_LEVELS:
        same = (r // (2 * w)) == (c // (2 * w))
        r_in_i = (r % (2 * w)) >= w
        c_in_i = (c % (2 * w)) >= w
        in_range = (r_in_i & c_in_i & (c <= r)) | ((~r_in_i) & (~c_in_i) & (c > r))
        m_arg = jnp.where(same & in_range, 1.0, 0.0).astype(BF16)
        levels.append((m_arg, same, (row % (2 * w)) >= w))
    eye = r == c

    def group(gi, carry):
        r0 = pl.multiple_of(gi * G, G)
        rows = pl.ds(r0, G)
        x = _dot(t_ref[rows, :].astype(BF16), w2_ref[...]) + b2_ref[...]
        gk = (jnp.minimum(x, 0.0) - jnp.log(1.0 + jnp.exp(-jnp.abs(x)))) * (1.0 / GATE_NORMALIZER)
        hi = gk.astype(BF16)
        lo = (gk - hi.astype(F32)).astype(BF16)
        b = _split_dot(tril2, hi, lo)
        q = q_ref[rows, :].astype(F32) * (DK_B ** -0.5)
        k = k_ref[rows, :].astype(F32)
        v = v_ref[rows, :]
        q_lv, k_lv, masks = [], [], []
        for m_arg, same, row_in_i in levels:
            e = jnp.exp(_split_dot(m_arg, hi, lo))
            q_lv.append(jnp.where(row_in_i, q * e, 0.0))
            k_lv.append(jnp.where(row_in_i, 0.0, k * e).astype(BF16))
            masks.append(same)
        q_lv.append(q)
        k_lv.append(k.astype(BF16))
        masks.append(eye)
        atts = [None] * H_B
        for ql, kl_, mask in zip(q_lv, k_lv, masks):
            for h in range(H_B):
                blk = slice((h // 2) * LANES, (h // 2 + 1) * LANES)
                qm = jnp.where(head_lanes[h % 2], ql[:, blk], 0.0).astype(BF16)
                part = jnp.where(mask, _dot_nt(qm, kl_[:, blk]), 0.0)
                atts[h] = part if atts[h] is None else atts[h] + part
        avs = [_dot(atts[h].astype(BF16), v[:, h * DV_B:(h + 1) * DV_B]) for h in range(H_B)]
        qe = q * jnp.exp(b)
        sts = [st_sc[p] for p in range(H_B // 2)]
        inters = [[] for _ in range(H_B)]
        for c2 in range(G // CHUNK):
            cs = slice(c2 * CHUNK, (c2 + 1) * CHUNK)
            for h in range(H_B):
                blk = slice((h // 2) * LANES, (h // 2 + 1) * LANES)
                qm = jnp.where(head_lanes_c[h % 2], qe[cs, blk], 0.0).astype(BF16)
                inters[h].append(_dot_nt(qm, sts[h // 2].astype(BF16)))
            bl = b[(c2 + 1) * CHUNK - 1:(c2 + 1) * CHUNK, :]
            kl = (k[cs] * jnp.exp(bl - b[cs])).astype(BF16)
            ebl = jnp.exp(bl)
            kvs = [lax.dot_general(v[cs, h * DV_B:(h + 1) * DV_B], kl[:, (h // 2) * LANES:(h // 2 + 1) * LANES],
                                   TN_DIMS, preferred_element_type=F32) for h in range(H_B)]
            sts = [sts[p] * ebl[:, p * LANES:(p + 1) * LANES]
                   + jnp.where(head_lanes[0], kvs[2 * p], kvs[2 * p + 1]) for p in range(H_B // 2)]
        for p in range(H_B // 2):
            st_sc[p] = sts[p]
        for h in range(H_B):
            vs = slice(h * DV_B, (h + 1) * DV_B)
            o = jnp.concatenate(inters[h], axis=0) + avs[h]
            ms = jnp.mean(o * o, axis=-1, keepdims=True)
            gate = r_ref[rows, vs].astype(F32)
            y = o * lax.rsqrt(ms + RMS_EPS) * g_ref[...] * (gate * _sigmoid(gate))
            o_ref[rows, vs] = y.astype(BF16)
        return carry

    lax.fori_loop(0, rb // G, group, 0)


def _gla(hp, tail, w2p, b2, norm_g, *, B, S, rb):
    nb = S // rb
    T = B * S
    kern = functools.partial(_gla_kernel, rb=rb)
    return pl.pallas_call(
        kern,
        grid=(B, nb),
        in_specs=[
            pl.BlockSpec((rb, H_B * DK_B), lambda b, i: (b * nb + i, 6)),
            pl.BlockSpec((rb, H_B * DK_B), lambda b, i: (b * nb + i, 7)),
            pl.BlockSpec((rb, H_B * DV_B), lambda b, i: (b * nb + i, 4)),
            pl.BlockSpec((rb, H_B * DV_B), lambda b, i: (b * nb + i, 5)),
            pl.BlockSpec((rb, LANES), lambda b, i: (b * nb + i, 0)),
            pl.BlockSpec((LANES, H_B * DK_B), lambda b, i: (0, 0)),
            pl.BlockSpec((1, H_B * DK_B), lambda b, i: (0, 0)),
            pl.BlockSpec((1, DV_B), lambda b, i: (0, 0)),
        ],
        out_specs=pl.BlockSpec((rb, H_B * DV_B), lambda b, i: (b * nb + i, 0)),
        out_shape=jax.ShapeDtypeStruct((T, H_B * DV_B), BF16),
        scratch_shapes=[pltpu.VMEM((H_B // 2, DV_B, 2 * DK_B), F32)],
        compiler_params=_cparams(("parallel", "arbitrary")),
        name="gla",
    )(hp, hp, hp, hp, tail, w2p, b2, norm_g)


GDN_GROUP = 2 * CHUNK


def _gdn_kernel(q_ref, k_ref, v_ref, z_ref, t_ref, hp_ref, g_ref, o_ref, s_sc, *, rb):
    G = GDN_GROUP

    @pl.when(pl.program_id(1) == 0)
    def _():
        s_sc[...] = jnp.zeros(s_sc.shape, F32)

    r = lax.broadcasted_iota(jnp.int32, (G, G), 0)
    c = lax.broadcasted_iota(jnp.int32, (G, G), 1)
    same = (r >= CHUNK) == (c >= CHUNK)
    incl = same & (c <= r)
    strict = same & (c < r)
    tril2 = jnp.where(incl, 1.0, 0.0).astype(BF16)
    eye = (r == c).astype(F32)

    def group(gi, carry):
        r0 = pl.multiple_of(gi * G, G)
        rows = pl.ds(r0, G)
        t = t_ref[rows, :]
        beta_all = _sigmoid(t)
        g_all = -jnp.exp(hp_ref[0:1, :]) * _softplus(t + hp_ref[1:2, :])
        g_hi = g_all.astype(BF16)
        g_rest = g_all - g_hi.astype(F32)
        g_mid = g_rest.astype(BF16)
        g_lo = (g_rest - g_mid.astype(F32)).astype(BF16)
        cum_all = _dot(tril2, g_hi) + _dot(tril2, g_mid) + _dot(tril2, g_lo)
        cum_all_t = cum_all.T
        heads = range(H_C)
        khbs = [k_ref[rows, h * LANES:(h + 1) * LANES] for h in heads]
        qhbs = [q_ref[rows, h * LANES:(h + 1) * LANES] for h in heads]
        qs = [q.astype(F32) for q in qhbs]
        ks = [k.astype(F32) for k in khbs]
        vs = [v_ref[rows, h * LANES:(h + 1) * LANES].astype(F32) for h in heads]
        bhs = [jnp.broadcast_to(beta_all[:, h:h + 1], (G, LANES)) for h in heads]
        chs = [jnp.broadcast_to(cum_all[:, H_C + h:H_C + h + 1], (G, LANES)) for h in heads]
        crows = [jnp.broadcast_to(cum_all_t[H_C + h:H_C + h + 1, :], (G, G)) for h in heads]
        decs = [jnp.where(incl, jnp.exp(ch - crow), 0.0) for ch, crow in zip(chs, crows)]
        egs = [jnp.exp(ch) for ch in chs]
        kbs = [k * bh for k, bh in zip(ks, bhs)]
        kks = [_dot_nt(kb.astype(BF16), khb) for kb, khb in zip(kbs, khbs)]
        ps = [-jnp.where(strict, kk * dec, 0.0) for kk, dec in zip(kks, decs)]
        tinvs = [eye + p for p in ps]
        for _ in range(5):
            pbs = [p.astype(BF16) for p in ps]
            ps = [_dot(pb, pb) for pb in pbs]
            tinvs = [tinv + _dot(p.astype(BF16), tinv.astype(BF16)) for p, tinv in zip(ps, tinvs)]
        rhss = [jnp.concatenate([v * bh, kb * eg], axis=1).astype(BF16)
                for v, bh, kb, eg in zip(vs, bhs, kbs, egs)]
        uws = [_dot(tinv.astype(BF16), rhs) for tinv, rhs in zip(tinvs, rhss)]
        qks = [_dot_nt(qb, khb) * dec for qb, khb, dec in zip(qhbs, khbs, decs)]
        qgs = [q * eg for q, eg in zip(qs, egs)]
        sts = [s_sc[h] for h in heads]
        outs = [[] for _ in heads]
        for c2 in range(G // CHUNK):
            cs = slice(c2 * CHUNK, (c2 + 1) * CHUNK)
            wss = [_dot(jnp.concatenate([uw[cs, DV_C:], qg[cs]], axis=0).astype(BF16), st.astype(BF16))
                   for uw, qg, st in zip(uws, qgs, sts)]
            v_news = [uw[cs, :DV_C] - ws[:CHUNK] for uw, ws in zip(uws, wss)]
            for h in heads:
                outs[h].append(wss[h][CHUNK:] + _dot(qks[h][cs, cs].astype(BF16), v_news[h].astype(BF16)))
            gls = [ch[(c2 + 1) * CHUNK - 1:(c2 + 1) * CHUNK, :] for ch in chs]
            kgs = [k[cs] * jnp.exp(gl - ch[cs]) for k, gl, ch in zip(ks, gls, chs)]
            sts = [st * jnp.exp(gl) + _dot(kg.T.astype(BF16), v_new.astype(BF16))
                   for st, gl, kg, v_new in zip(sts, gls, kgs, v_news)]
        for h in heads:
            ls = slice(h * LANES, (h + 1) * LANES)
            s_sc[h] = sts[h]
            o = jnp.concatenate(outs[h], axis=0)
            ms = jnp.mean(o * o, axis=-1, keepdims=True)
            z = z_ref[pl.ds(r0, G), ls].astype(F32)
            y = o * lax.rsqrt(ms + RMS_EPS) * g_ref[...] * (z * _sigmoid(z))
            o_ref[pl.ds(r0, G), ls] = y.astype(BF16)
        return carry

    lax.fori_loop(0, rb // G, group, 0)


def _gdn(hp, tail, head_par, norm_g, *, B, S, rb):
    nb = S // rb
    T = B * S
    width = H_C * DK_C
    kern = functools.partial(_gdn_kernel, rb=rb)
    return pl.pallas_call(
        kern,
        grid=(B, nb),
        in_specs=[
            pl.BlockSpec((rb, width), lambda b, i: (b * nb + i, 0)),
            pl.BlockSpec((rb, width), lambda b, i: (b * nb + i, 1)),
            pl.BlockSpec((rb, width), lambda b, i: (b * nb + i, 2)),
            pl.BlockSpec((rb, width), lambda b, i: (b * nb + i, 3)),
            pl.BlockSpec((rb, LANES), lambda b, i: (b * nb + i, 0)),
            pl.BlockSpec((8, LANES), lambda b, i: (0, 0)),
            pl.BlockSpec((1, DV_C), lambda b, i: (0, 0)),
        ],
        out_specs=pl.BlockSpec((rb, width), lambda b, i: (b * nb + i, 0)),
        out_shape=jax.ShapeDtypeStruct((T, width), BF16),
        scratch_shapes=[pltpu.VMEM((H_C, DK_C, DV_C), F32)],
        compiler_params=_cparams(("parallel", "arbitrary")),
        name="gdn",
    )(hp, hp, hp, hp, tail, head_par, norm_g)


N_POST_PARAMS = 15


def _post_kernel(*refs, n_in, tm, tiles_per_seq, emit_bf16):
    ins = refs[:n_in]
    (wmix_ref, res_ref, g1_ref, b1_ref, kv_ref, wq_ref, wo_ref, g2_ref, b2_ref,
     win_ref, cw_ref, cb_ref, wout_ref, g3_ref, b3_ref) = refs[n_in:n_in + N_POST_PARAMS]
    n_out = 2 if emit_bf16 else 1
    outs = refs[n_in + N_POST_PARAMS:n_in + N_POST_PARAMS + n_out]
    att_sc, xcat_sc, hid_sc, act_sc = refs[n_in + N_POST_PARAMS + n_out:]

    y = ALPHA * res_ref[...]
    off = 0
    for r in ins:
        kdim = r.shape[1]
        y = y + _dot(r[...], wmix_ref[off:off + kdim, :])
        off += kdim
    h1 = _layer_norm(y, g1_ref[...], b1_ref[...])

    q = (_dot(h1.astype(BF16), wq_ref[...]) * (D_XA ** -0.5)).astype(BF16)
    for hd in range(N_XA):
        cs = slice(hd * D_XA, (hd + 1) * D_XA)
        s = _dot_nt(q[:, cs], kv_ref[:, cs])
        p = jnp.exp(s - jnp.max(s, axis=-1, keepdims=True))
        l = jnp.sum(p, axis=-1, keepdims=True)
        vs = slice(D_MODEL + hd * D_XA, D_MODEL + (hd + 1) * D_XA)
        att_sc[:, cs] = (_dot(p.astype(BF16), kv_ref[:, vs]) / l).astype(BF16)
    h2 = _layer_norm(ALPHA * h1 + _dot(att_sc[...], wo_ref[...]), g2_ref[...], b2_ref[...])

    is_start = (pl.program_id(0) % tiles_per_seq) == 0

    @pl.when(is_start)
    def _():
        xcat_sc[0:HALO, :] = jnp.zeros((HALO, D_MODEL), BF16)

    @pl.when(jnp.logical_not(is_start))
    def _():
        xcat_sc[0:HALO, :] = xcat_sc[tm:tm + HALO, :]

    xcat_sc[HALO:HALO + tm, :] = h2.astype(BF16)
    xc = xcat_sc[...]

    def conv(col):
        hid_sc[...] = _dot(xc, win_ref[:, col:col + FF_CHUNK])
        w = cw_ref[:, col:col + FF_CHUNK]
        y = cb_ref[:, col:col + FF_CHUNK] + w[FFN_CONV - 1:FFN_CONV, :] * hid_sc[HALO:HALO + tm, :]
        for j in range(FFN_CONV - 1):
            o = HALO - (FFN_CONV - 1) + j
            y = y + w[j:j + 1, :] * hid_sc[o:o + tm, :]
        return y

    for c in range(N_FF_CHUNKS):
        gate = conv(c * FF_CHUNK)
        up = conv(D_FF + c * FF_CHUNK)
        act_sc[:, c * FF_CHUNK:(c + 1) * FF_CHUNK] = (gate * _sigmoid(gate) * up).astype(BF16)
    out = _layer_norm(ALPHA * h2 + _dot(act_sc[...], wout_ref[...]), g3_ref[...], b3_ref[...])
    outs[0][...] = out
    if emit_bf16:
        outs[1][...] = out.astype(BF16)


def _post(ins, res, kv, params, *, S, tm, emit_bf16):
    (w_mix, g1, b1, wq, wo, g2, b2, w_in, cw, cb, w_out, g3, b3) = params
    T = res.shape[0]
    tiles_per_seq = S // tm
    kern = functools.partial(_post_kernel, n_in=len(ins), tm=tm, tiles_per_seq=tiles_per_seq, emit_bf16=emit_bf16)
    row_tile = pl.BlockSpec((tm, D_MODEL), lambda i: (i, 0))
    vec = pl.BlockSpec((1, D_MODEL), lambda i: (0, 0))
    in_specs = [pl.BlockSpec((tm, a.shape[1]), lambda i: (i, 0)) for a in ins]
    in_specs += [
        _const_spec(w_mix.shape), row_tile, vec, vec,
        pl.BlockSpec((N_MEM, 2 * D_MODEL), lambda i: (i // tiles_per_seq, 0)),
        _const_spec(wq.shape), _const_spec(wo.shape), vec, vec,
        _const_spec(w_in.shape), _const_spec(cw.shape), _const_spec(cb.shape), _const_spec(w_out.shape), vec, vec,
    ]
    out_specs = [row_tile]
    out_shape = [jax.ShapeDtypeStruct((T, D_MODEL), F32)]
    if emit_bf16:
        out_specs.append(row_tile)
        out_shape.append(jax.ShapeDtypeStruct((T, D_MODEL), BF16))
    return pl.pallas_call(
        kern,
        grid=(T // tm,),
        in_specs=in_specs,
        out_specs=out_specs,
        out_shape=out_shape,
        scratch_shapes=[
            pltpu.VMEM((tm, D_MODEL), BF16),
            pltpu.VMEM((tm + HALO, D_MODEL), BF16),
            pltpu.VMEM((tm + HALO, FF_CHUNK), F32),
            pltpu.VMEM((tm, D_FF), BF16),
        ],
        compiler_params=_cparams(("arbitrary",)),
        name="post",
    )(*ins, w_mix, res, g1, b1, kv, wq, wo, g2, b2, w_in, cw, cb, w_out, g3, b3)


def _rope_consts():
    inv_freq = ROPE_THETA ** (-jnp.arange(0, ROPE_DIM, 2, dtype=F32) / ROPE_DIM)
    d = jnp.arange(LANES) % D_A
    half = ROPE_DIM // 2
    freq = jnp.where(d < ROPE_DIM, inv_freq[d % half], 0.0)
    lo = jnp.where(d < half, -1.0, 0.0)
    hi = jnp.where((d >= half) & (d < ROPE_DIM), 1.0, 0.0)
    return jnp.zeros((8, LANES), F32).at[0].set(freq).at[1].set(lo).at[2].set(hi)


def _row(v):
    return v.reshape(1, -1).astype(F32)


def _tail_w(w, n_main):
    tail = w[:, n_main:]
    return jnp.pad(tail, ((0, 0), (0, LANES - tail.shape[1]))).astype(BF16)


def _post_layer(ins, res, mem2d, S, params, tm, emit_bf16):
    (w_mix, ln1_g, ln1_b, xa_wq, xa_wkv, xa_wo, ln2_g, ln2_b, f_in, f_cw, f_cb, f_out, ln3_g, ln3_b) = params
    kv = _kvproj(mem2d, xa_wkv.astype(BF16), tm=tm)
    prepared = (w_mix.astype(BF16), _row(ln1_g), _row(ln1_b), xa_wq.astype(BF16), xa_wo.astype(BF16),
                _row(ln2_g), _row(ln2_b), f_in.astype(BF16), f_cw.astype(F32), _row(f_cb), f_out.astype(BF16),
                _row(ln3_g), _row(ln3_b))
    return _post(ins, res, kv, prepared, S=S, tm=tm, emit_bf16=emit_bf16)


def kernel(x, mem, positions, w_in_0, diff_lambda_0, diff_norm_0, gla_w2_0, gla_b2_0, gla_norm_0, w_mix_out_0, ln1_g_0, ln1_b_0, xa_wq_0, xa_wkv_0, xa_wo_0, ln2_g_0, ln2_b_0, ffn_w_in_0, ffn_conv_w_0, ffn_conv_b_0, ffn_w_out_0, ln3_g_0, ln3_b_0, w_in_1, gdn_conv_w_1, gdn_a_log_1, gdn_dt_bias_1, gdn_norm_1, w_mix_out_1, ln1_g_1, ln1_b_1, xa_wq_1, xa_wkv_1, xa_wo_1, ln2_g_1, ln2_b_1, ffn_w_in_1, ffn_conv_w_1, ffn_conv_b_1, ffn_w_out_1, ln3_g_1, ln3_b_1):
    B, S, _ = x.shape
    T = B * S
    tm = min(512, S)
    x2d = x.reshape(T, D_MODEL)
    mem2d = mem.reshape(B * N_MEM, D_MODEL)

    hp, tail = _proj0(x2d, positions.reshape(T, 1), _rope_consts(),
                      w_in_0[:, :P0_MAIN].astype(BF16), _tail_w(w_in_0, P0_MAIN), tm=tm)
    lam_init = 0.8 - 0.6 * math.exp(-0.3 * 0)
    oa = _diffattn(hp, diff_lambda_0.astype(F32), _row(diff_norm_0), B=B, S=S, tq=min(256, S), lam_init=lam_init)
    w2p = jnp.pad(gla_w2_0, ((0, LANES - GATE_RANK), (0, 0))).astype(BF16)
    ob = _gla(hp, tail, w2p, _row(gla_b2_0), _row(gla_norm_0), B=B, S=S, rb=min(512, S))
    h, hb = _post_layer([oa, ob], x2d, mem2d, S,
                        (w_mix_out_0, ln1_g_0, ln1_b_0, xa_wq_0, xa_wkv_0, xa_wo_0, ln2_g_0, ln2_b_0,
                         ffn_w_in_0, ffn_conv_w_0, ffn_conv_b_0, ffn_w_out_0, ln3_g_0, ln3_b_0), tm, True)

    hp, tail = _proj1(hb, w_in_1[:, :P1_MAIN].astype(BF16), _tail_w(w_in_1, P1_MAIN), gdn_conv_w_1.astype(F32),
                      S=S, tm=tm)
    head_par = (jnp.zeros((8, LANES), F32)
                .at[0, H_C:2 * H_C].set(gdn_a_log_1.astype(F32))
                .at[1, H_C:2 * H_C].set(gdn_dt_bias_1.astype(F32)))
    oc = _gdn(hp, tail, head_par, _row(gdn_norm_1), B=B, S=S, rb=min(512, S))
    (out,) = _post_layer([oc], h, mem2d, S,
                         (w_mix_out_1, ln1_g_1, ln1_b_1, xa_wq_1, xa_wkv_1, xa_wo_1, ln2_g_1, ln2_b_1,
                          ffn_w_in_1, ffn_conv_w_1, ffn_conv_b_1, ffn_w_out_1, ln3_g_1, ln3_b_1), tm, False)
    return out.reshape(B, S, D_MODEL)
```

```python
import functools
import math

import jax
import jax.numpy as jnp
from jax import lax
from jax.experimental import pallas as pl
from jax.experimental.pallas import tpu as pltpu

F32 = jnp.float32
BF16 = jnp.bfloat16

D_MODEL = 1024
N_MEM = 256
H_A, D_A, DV_A = 4, 64, 128
ROPE_THETA, ROPE_DIM = 500000.0, 16
H_B, DK_B, DV_B = 4, 64, 128
GATE_RANK, GATE_NORMALIZER = 16, 16.0
H_C, DK_C, DV_C, CONV_C = 8, 128, 128, 4
CHUNK = 64
N_XA, D_XA = 4, 256
D_FF = 2816
FFN_CONV = 3
LN_EPS, RMS_EPS = 1e-5, 1e-6
DEPTH = 2
ALPHA = (2.0 * DEPTH) ** 0.25
P0_MAIN = 3072
P1_MAIN = 4096
LANES = 128
FF_CHUNK = 256
N_FF_CHUNKS = D_FF // FF_CHUNK
HALO = 16
VMEM_LIMIT = 56 * 1024 * 1024

NT_DIMS = (((1,), (1,)), ((), ()))
TN_DIMS = (((0,), (0,)), ((), ()))
HI = lax.Precision.HIGHEST


def _cparams(sem):
    return pltpu.CompilerParams(dimension_semantics=sem, vmem_limit_bytes=VMEM_LIMIT)


def _const_spec(shape):
    nd = len(shape)
    return pl.BlockSpec(shape, lambda *_: (0,) * nd, pipeline_mode=pl.Buffered(1))


def _dot(a, b):
    return jnp.dot(a, b, preferred_element_type=F32)


def _dot_nt(a, b, precision=None):
    return lax.dot_general(a, b, NT_DIMS, precision=precision, preferred_element_type=F32)


def _layer_norm(y, g, b):
    mu = jnp.mean(y, axis=-1, keepdims=True)
    yc = y - mu
    var = jnp.mean(yc * yc, axis=-1, keepdims=True)
    return yc * lax.rsqrt(var + LN_EPS) * g + b


def _sigmoid(x):
    return 1.0 / (1.0 + jnp.exp(-x))


def _softplus(x):
    return jnp.maximum(x, 0.0) + jnp.log(1.0 + jnp.exp(-jnp.abs(x)))


def _tril_mask(n, strict=False):
    r = lax.broadcasted_iota(jnp.int32, (n, n), 0)
    c = lax.broadcasted_iota(jnp.int32, (n, n), 1)
    return (c < r) if strict else (c <= r)


def _proj0_kernel(x_ref, pos_ref, rc_ref, w_ref, wt_ref, o_ref, t_ref):
    xb = x_ref[...].astype(BF16)
    ang = pos_ref[...].astype(F32) * rc_ref[0:1, :]
    c = jnp.cos(ang)
    s = jnp.sin(ang)
    s_lo = s * rc_ref[1:2, :]
    s_hi = s * rc_ref[2:3, :]
    q_scale = D_A ** -0.5
    for j in range(2):
        y = _dot(xb, w_ref[:, j * 512:(j + 1) * 512])
        for t in range(4):
            yt = y[:, t * LANES:(t + 1) * LANES]
            yt = yt * c + pltpu.roll(yt, LANES - 8, 1) * s_lo + pltpu.roll(yt, 8, 1) * s_hi
            if j == 0:
                yt = yt * q_scale
            col = j * 512 + t * LANES
            o_ref[:, col:col + LANES] = yt.astype(BF16)
    for col in range(1024, P0_MAIN, 512):
        o_ref[:, col:col + 512] = _dot(xb, w_ref[:, col:col + 512]).astype(BF16)
    t_ref[...] = _dot(xb, wt_ref[...])


def _proj0(x2d, pos2d, rope_c, w_main, w_tail, *, tm):
    T = x2d.shape[0]
    return pl.pallas_call(
        _proj0_kernel,
        grid=(T // tm,),
        in_specs=[
            pl.BlockSpec((tm, D_MODEL), lambda i: (i, 0)),
            pl.BlockSpec((tm, 1), lambda i: (i, 0)),
            _const_spec((8, LANES)),
            _const_spec((D_MODEL, P0_MAIN)),
            _const_spec((D_MODEL, LANES)),
        ],
        out_specs=[
            pl.BlockSpec((tm, P0_MAIN), lambda i: (i, 0)),
            pl.BlockSpec((tm, LANES), lambda i: (i, 0)),
        ],
        out_shape=[
            jax.ShapeDtypeStruct((T, P0_MAIN), BF16),
            jax.ShapeDtypeStruct((T, LANES), F32),
        ],
        compiler_params=_cparams(("parallel",)),
        name="proj0",
    )(x2d, pos2d, rope_c, w_main, w_tail)


QKV_C = 3 * H_C * DK_C
PROJ_CHUNK = 512


def _proj1_kernel(x_ref, w_ref, wt_ref, cw_ref, o_ref, t_ref, ybuf_sc, carry_sc, *, tm, tiles_per_seq):
    xb = x_ref[...]

    @pl.when((pl.program_id(0) % tiles_per_seq) == 0)
    def _():
        carry_sc[...] = jnp.zeros(carry_sc.shape, F32)

    def plain(col):
        o_ref[:, col:col + PROJ_CHUNK] = _dot(xb, w_ref[:, col:col + PROJ_CHUNK]).astype(BF16)

    def tail():
        t_ref[...] = _dot(xb, wt_ref[...])

    fillers = [functools.partial(plain, col) for col in range(QKV_C, P1_MAIN, PROJ_CHUNK)] + [tail]
    for ci, col in enumerate(range(0, QKV_C, PROJ_CHUNK)):
        ybuf_sc[0:8, :] = carry_sc[ci]
        ybuf_sc[8:8 + tm, :] = _dot(xb, w_ref[:, col:col + PROJ_CHUNK])
        carry_sc[ci] = ybuf_sc[tm:tm + 8, :]
        if ci % 2 == 0 and fillers:
            fillers.pop(0)()
        w = cw_ref[:, col:col + PROJ_CHUNK]
        y = w[CONV_C - 1:CONV_C, :] * ybuf_sc[8:8 + tm, :]
        for j in range(CONV_C - 1):
            y = y + w[j:j + 1, :] * ybuf_sc[5 + j:5 + j + tm, :]
        y = y * _sigmoid(y)
        if col < 2 * H_C * DK_C:
            scale = DK_C ** -0.5 if col < H_C * DK_C else 1.0
            for t in range(PROJ_CHUNK // LANES):
                yt = y[:, t * LANES:(t + 1) * LANES]
                yt = yt * (lax.rsqrt(jnp.sum(yt * yt, axis=-1, keepdims=True) + RMS_EPS) * scale)
                o_ref[:, col + t * LANES:col + (t + 1) * LANES] = yt.astype(BF16)
        else:
            o_ref[:, col:col + PROJ_CHUNK] = y.astype(BF16)
    for f in fillers:
        f()


def _proj1(xb2d, w_main, w_tail, conv_w, *, S, tm):
    T = xb2d.shape[0]
    kern = functools.partial(_proj1_kernel, tm=tm, tiles_per_seq=S // tm)
    return pl.pallas_call(
        kern,
        grid=(T // tm,),
        in_specs=[
            pl.BlockSpec((tm, D_MODEL), lambda i: (i, 0)),
            _const_spec((D_MODEL, P1_MAIN)),
            _const_spec((D_MODEL, LANES)),
            _const_spec((CONV_C, QKV_C)),
        ],
        out_specs=[
            pl.BlockSpec((tm, P1_MAIN), lambda i: (i, 0)),
            pl.BlockSpec((tm, LANES), lambda i: (i, 0)),
        ],
        out_shape=[
            jax.ShapeDtypeStruct((T, P1_MAIN), BF16),
            jax.ShapeDtypeStruct((T, LANES), F32),
        ],
        scratch_shapes=[
            pltpu.VMEM((tm + 8, PROJ_CHUNK), F32),
            pltpu.VMEM((QKV_C // PROJ_CHUNK, 8, PROJ_CHUNK), F32),
        ],
        compiler_params=_cparams(("arbitrary",)),
        name="proj1",
    )(xb2d, w_main, w_tail, conv_w)


def _kvproj_kernel(m_ref, w_ref, o_ref):
    mb = m_ref[...].astype(BF16)
    for col in range(0, 2 * D_MODEL, 512):
        o_ref[:, col:col + 512] = _dot(mb, w_ref[:, col:col + 512]).astype(BF16)


def _kvproj(mem2d, wkv, *, tm):
    R = mem2d.shape[0]
    return pl.pallas_call(
        _kvproj_kernel,
        grid=(R // tm,),
        in_specs=[
            pl.BlockSpec((tm, D_MODEL), lambda i: (i, 0)),
            _const_spec((D_MODEL, 2 * D_MODEL)),
        ],
        out_specs=pl.BlockSpec((tm, 2 * D_MODEL), lambda i: (i, 0)),
        out_shape=jax.ShapeDtypeStruct((R, 2 * D_MODEL), BF16),
        compiler_params=_cparams(("parallel",)),
        name="kvproj",
    )(mem2d, wkv)


def _diffattn_kernel(q_ref, k_ref, v_ref, lam_ref, g_ref, o_ref, *, S, tq, lam_init):
    lv = lam_ref[...]
    lam = (jnp.exp(jnp.sum(lv[0:1] * lv[1:2], axis=-1, keepdims=True))
           - jnp.exp(jnp.sum(lv[2:3] * lv[3:4], axis=-1, keepdims=True)) + lam_init)
    lane = lax.broadcasted_iota(jnp.int32, (tq, LANES), 1)
    r = lax.broadcasted_iota(jnp.int32, (2 * tq, tq), 0)
    c = lax.broadcasted_iota(jnp.int32, (2 * tq, tq), 1)
    causal = c <= jnp.where(r >= tq, r - tq, r)
    vt = v_ref[...].astype(F32).T
    vt_ext = jnp.concatenate([vt, jnp.ones((HALO, S), F32)], axis=0).astype(BF16)
    g_col = g_ref[...]
    def scores(qi):
        n_keys = (qi + 1) * tq
        q = q_ref[qi * tq:(qi + 1) * tq, :]
        zero = jnp.zeros_like(q)
        qq = jnp.concatenate([jnp.where(lane < D_A, q, zero), jnp.where(lane >= D_A, q, zero)], axis=0)
        s = _dot_nt(qq, k_ref[0:n_keys, :])
        s_diag = jnp.where(causal, s[:, n_keys - tq:], -jnp.inf)
        return s_diag if qi == 0 else jnp.concatenate([s[:, :n_keys - tq], s_diag], axis=1)

    def softmax_num(s):
        return jnp.exp(s - jnp.max(s, axis=-1, keepdims=True)).astype(BF16)

    def finish(qi, p):
        n_keys = (qi + 1) * tq
        ot = _dot_nt(vt_ext[:, 0:n_keys], p)
        o_all = ot[:DV_A] / ot[DV_A:DV_A + 1]
        o = o_all[:, :tq] - lam * o_all[:, tq:]
        ms = jnp.mean(o * o, axis=0, keepdims=True)
        y = o * lax.rsqrt(ms + RMS_EPS) * g_col * (1.0 - lam_init)
        o_ref[qi * tq:(qi + 1) * tq, :] = y.T.astype(BF16)

    nq = S // tq
    pairs = [(i, nq - 1 - i) for i in range(nq // 2)] + ([(nq // 2,)] if nq % 2 else [])
    order = [sum(pairs[i:i + 2], ()) for i in range(0, len(pairs), 2)]
    for group in order:
        ss = [scores(qi) for qi in group]
        ps = [softmax_num(s) for s in ss]
        for qi, p in zip(group, ps):
            finish(qi, p)


def _diffattn(hp, lam_vecs, norm_g, *, B, S, tq, lam_init):
    T = B * S
    kern = functools.partial(_diffattn_kernel, S=S, tq=tq, lam_init=lam_init)
    return pl.pallas_call(
        kern,
        grid=(B, H_A),
        in_specs=[
            pl.BlockSpec((S, LANES), lambda b, h: (b, h)),
            pl.BlockSpec((S, LANES), lambda b, h: (b, H_A + h)),
            pl.BlockSpec((S, LANES), lambda b, h: (b, 2 * H_A + h)),
            pl.BlockSpec((4, D_A), lambda b, h: (0, 0)),
            pl.BlockSpec((DV_A, 1), lambda b, h: (0, 0)),
        ],
        out_specs=pl.BlockSpec((S, LANES), lambda b, h: (b, h)),
        out_shape=jax.ShapeDtypeStruct((T, H_A * DV_A), BF16),
        compiler_params=_cparams(("parallel", "parallel")),
        name="diffattn",
    )(hp, hp, hp, lam_vecs, norm_g)


GLA_GROUP = 2 * CHUNK
GLA_LEVELS = (32, 16, 8, 4, 2, 1)


def _split_dot(m_bf16, hi, lo):
    return _dot(m_bf16, hi) + _dot(m_bf16, lo)


def _gla_kernel(q_ref, k_ref, v_ref, r_ref, t_ref, w2_ref, b2_ref, g_ref, o_ref, st_sc, *, rb):
    G = GLA_GROUP

    @pl.when(pl.program_id(1) == 0)
    def _():
        st_sc[...] = jnp.zeros(st_sc.shape, F32)

    r = lax.broadcasted_iota(jnp.int32, (G, G), 0)
    c = lax.broadcasted_iota(jnp.int32, (G, G), 1)
    tril2 = jnp.where(((r >= CHUNK) == (c >= CHUNK)) & (c <= r), 1.0, 0.0).astype(BF16)
    row = lax.broadcasted_iota(jnp.int32, (G, H_B * DK_B), 0)
    lane = lax.broadcasted_iota(jnp.int32, (G, LANES), 1)
    head_lanes = [lane < DK_B, lane >= DK_B]
    lane_c = lax.broadcasted_iota(jnp.int32, (CHUNK, LANES), 1)
    head_lanes_c = [lane_c < DK_B, lane_c >= DK_B]
    levels = []
    for w in GLA_LEVELS:
        same = (r // (2 * w)) == (c // (2 * w))
        r_in_i = (r % (2 * w)) >= w
        c_in_i = (c % (2 * w)) >= w
        in_range = (r_in_i & c_in_i & (c <= r)) | ((~r_in_i) & (~c_in_i) & (c > r))
        m_arg = jnp.where(same & in_range, 1.0, 0.0).astype(BF16)
        levels.append((m_arg, same, (row % (2 * w)) >= w))
    eye = r == c

    def group(gi, carry):
        r0 = pl.multiple_of(gi * G, G)
        rows = pl.ds(r0, G)
        x = _dot(t_ref[rows, :].astype(BF16), w2_ref[...]) + b2_ref[...]
        gk = (jnp.minimum(x, 0.0) - jnp.log(1.0 + jnp.exp(-jnp.abs(x)))) * (1.0 / GATE_NORMALIZER)
        hi = gk.astype(BF16)
        lo = (gk - hi.astype(F32)).astype(BF16)
        b = _split_dot(tril2, hi, lo)
        q = q_ref[rows, :].astype(F32) * (DK_B ** -0.5)
        k = k_ref[rows, :].astype(F32)
        v = v_ref[rows, :]
        q_lv, k_lv, masks = [], [], []
        for m_arg, same, row_in_i in levels:
            e = jnp.exp(_split_dot(m_arg, hi, lo))
            q_lv.append(jnp.where(row_in_i, q * e, 0.0))
            k_lv.append(jnp.where(row_in_i, 0.0, k * e).astype(BF16))
            masks.append(same)
        q_lv.append(q)
        k_lv.append(k.astype(BF16))
        masks.append(eye)
        atts = [None] * H_B
        for ql, kl_, mask in zip(q_lv, k_lv, masks):
            for h in range(H_B):
                blk = slice((h // 2) * LANES, (h // 2 + 1) * LANES)
                qm = jnp.where(head_lanes[h % 2], ql[:, blk], 0.0).astype(BF16)
                part = jnp.where(mask, _dot_nt(qm, kl_[:, blk]), 0.0)
                atts[h] = part if atts[h] is None else atts[h] + part
        avs = [_dot(atts[h].astype(BF16), v[:, h * DV_B:(h + 1) * DV_B]) for h in range(H_B)]
        qe = q * jnp.exp(b)
        sts = [st_sc[p] for p in range(H_B // 2)]
        inters = [[] for _ in range(H_B)]
        for c2 in range(G // CHUNK):
            cs = slice(c2 * CHUNK, (c2 + 1) * CHUNK)
            for h in range(H_B):
                blk = slice((h // 2) * LANES, (h // 2 + 1) * LANES)
                qm = jnp.where(head_lanes_c[h % 2], qe[cs, blk], 0.0).astype(BF16)
                inters[h].append(_dot_nt(qm, sts[h // 2].astype(BF16)))
            bl = b[(c2 + 1) * CHUNK - 1:(c2 + 1) * CHUNK, :]
            kl = (k[cs] * jnp.exp(bl - b[cs])).astype(BF16)
            ebl = jnp.exp(bl)
            kvs = [lax.dot_general(v[cs, h * DV_B:(h + 1) * DV_B], kl[:, (h // 2) * LANES:(h // 2 + 1) * LANES],
                                   TN_DIMS, preferred_element_type=F32) for h in range(H_B)]
            sts = [sts[p] * ebl[:, p * LANES:(p + 1) * LANES]
                   + jnp.where(head_lanes[0], kvs[2 * p], kvs[2 * p + 1]) for p in range(H_B // 2)]
        for p in range(H_B // 2):
            st_sc[p] = sts[p]
        for h in range(H_B):
            vs = slice(h * DV_B, (h + 1) * DV_B)
            o = jnp.concatenate(inters[h], axis=0) + avs[h]
            ms = jnp.mean(o * o, axis=-1, keepdims=True)
            gate = r_ref[rows, vs].astype(F32)
            y = o * lax.rsqrt(ms + RMS_EPS) * g_ref[...] * (gate * _sigmoid(gate))
            o_ref[rows, vs] = y.astype(BF16)
        return carry

    lax.fori_loop(0, rb // G, group, 0)


def _gla(hp, tail, w2p, b2, norm_g, *, B, S, rb):
    nb = S // rb
    T = B * S
    kern = functools.partial(_gla_kernel, rb=rb)
    return pl.pallas_call(
        kern,
        grid=(B, nb),
        in_specs=[
            pl.BlockSpec((rb, H_B * DK_B), lambda b, i: (b * nb + i, 6)),
            pl.BlockSpec((rb, H_B * DK_B), lambda b, i: (b * nb + i, 7)),
            pl.BlockSpec((rb, H_B * DV_B), lambda b, i: (b * nb + i, 4)),
            pl.BlockSpec((rb, H_B * DV_B), lambda b, i: (b * nb + i, 5)),
            pl.BlockSpec((rb, LANES), lambda b, i: (b * nb + i, 0)),
            pl.BlockSpec((LANES, H_B * DK_B), lambda b, i: (0, 0)),
            pl.BlockSpec((1, H_B * DK_B), lambda b, i: (0, 0)),
            pl.BlockSpec((1, DV_B), lambda b, i: (0, 0)),
        ],
        out_specs=pl.BlockSpec((rb, H_B * DV_B), lambda b, i: (b * nb + i, 0)),
        out_shape=jax.ShapeDtypeStruct((T, H_B * DV_B), BF16),
        scratch_shapes=[pltpu.VMEM((H_B // 2, DV_B, 2 * DK_B), F32)],
        compiler_params=_cparams(("parallel", "arbitrary")),
        name="gla",
    )(hp, hp, hp, hp, tail, w2p, b2, norm_g)


GDN_GROUP = 2 * CHUNK


def _gdn_kernel(q_ref, k_ref, v_ref, z_ref, t_ref, hp_ref, g_ref, o_ref, s_sc, *, rb):
    G = GDN_GROUP

    @pl.when(pl.program_id(1) == 0)
    def _():
        s_sc[...] = jnp.zeros(s_sc.shape, F32)

    r = lax.broadcasted_iota(jnp.int32, (G, G), 0)
    c = lax.broadcasted_iota(jnp.int32, (G, G), 1)
    same = (r >= CHUNK) == (c >= CHUNK)
    incl = same & (c <= r)
    strict = same & (c < r)
    tril2 = jnp.where(incl, 1.0, 0.0).astype(BF16)
    eye = (r == c).astype(F32)

    def group(gi, carry):
        r0 = pl.multiple_of(gi * G, G)
        rows = pl.ds(r0, G)
        t = t_ref[rows, :]
        beta_all = _sigmoid(t)
        g_all = -jnp.exp(hp_ref[0:1, :]) * _softplus(t + hp_ref[1:2, :])
        g_hi = g_all.astype(BF16)
        g_rest = g_all - g_hi.astype(F32)
        g_mid = g_rest.astype(BF16)
        g_lo = (g_rest - g_mid.astype(F32)).astype(BF16)
        cum_all = _dot(tril2, g_hi) + _dot(tril2, g_mid) + _dot(tril2, g_lo)
        cum_all_t = cum_all.T
        heads = range(H_C)
        khbs = [k_ref[rows, h * LANES:(h + 1) * LANES] for h in heads]
        qhbs = [q_ref[rows, h * LANES:(h + 1) * LANES] for h in heads]
        qs = [q.astype(F32) for q in qhbs]
        ks = [k.astype(F32) for k in khbs]
        vs = [v_ref[rows, h * LANES:(h + 1) * LANES].astype(F32) for h in heads]
        bhs = [jnp.broadcast_to(beta_all[:, h:h + 1], (G, LANES)) for h in heads]
        chs = [jnp.broadcast_to(cum_all[:, H_C + h:H_C + h + 1], (G, LANES)) for h in heads]
        crows = [jnp.broadcast_to(cum_all_t[H_C + h:H_C + h + 1, :], (G, G)) for h in heads]
        decs = [jnp.where(incl, jnp.exp(ch - crow), 0.0) for ch, crow in zip(chs, crows)]
        egs = [jnp.exp(ch) for ch in chs]
        kbs = [k * bh for k, bh in zip(ks, bhs)]
        kks = [_dot_nt(kb.astype(BF16), khb) for kb, khb in zip(kbs, khbs)]
        ps = [-jnp.where(strict, kk * dec, 0.0) for kk, dec in zip(kks, decs)]
        tinvs = [eye + p for p in ps]
        for _ in range(5):
            pbs = [p.astype(BF16) for p in ps]
            ps = [_dot(pb, pb) for pb in pbs]
            tinvs = [tinv + _dot(p.astype(BF16), tinv.astype(BF16)) for p, tinv in zip(ps, tinvs)]
        rhss = [jnp.concatenate([v * bh, kb * eg], axis=1).astype(BF16)
                for v, bh, kb, eg in zip(vs, bhs, kbs, egs)]
        uws = [_dot(tinv.astype(BF16), rhs) for tinv, rhs in zip(tinvs, rhss)]
        qks = [_dot_nt(qb, khb) * dec for qb, khb, dec in zip(qhbs, khbs, decs)]
        qgs = [q * eg for q, eg in zip(qs, egs)]
        sts = [s_sc[h] for h in heads]
        outs = [[] for _ in heads]
        for c2 in range(G // CHUNK):
            cs = slice(c2 * CHUNK, (c2 + 1) * CHUNK)
            wss = [_dot(jnp.concatenate([uw[cs, DV_C:], qg[cs]], axis=0).astype(BF16), st.astype(BF16))
                   for uw, qg, st in zip(uws, qgs, sts)]
            v_news = [uw[cs, :DV_C] - ws[:CHUNK] for uw, ws in zip(uws, wss)]
            for h in heads:
                outs[h].append(wss[h][CHUNK:] + _dot(qks[h][cs, cs].astype(BF16), v_news[h].astype(BF16)))
            gls = [ch[(c2 + 1) * CHUNK - 1:(c2 + 1) * CHUNK, :] for ch in chs]
            kgs = [k[cs] * jnp.exp(gl - ch[cs]) for k, gl, ch in zip(ks, gls, chs)]
            sts = [st * jnp.exp(gl) + _dot(kg.T.astype(BF16), v_new.astype(BF16))
                   for st, gl, kg, v_new in zip(sts, gls, kgs, v_news)]
        for h in heads:
            ls = slice(h * LANES, (h + 1) * LANES)
            s_sc[h] = sts[h]
            o = jnp.concatenate(outs[h], axis=0)
            ms = jnp.mean(o * o, axis=-1, keepdims=True)
            z = z_ref[pl.ds(r0, G), ls].astype(F32)
            y = o * lax.rsqrt(ms + RMS_EPS) * g_ref[...] * (z * _sigmoid(z))
            o_ref[pl.ds(r0, G), ls] = y.astype(BF16)
        return carry

    lax.fori_loop(0, rb // G, group, 0)


def _gdn(hp, tail, head_par, norm_g, *, B, S, rb):
    nb = S // rb
    T = B * S
    width = H_C * DK_C
    kern = functools.partial(_gdn_kernel, rb=rb)
    return pl.pallas_call(
        kern,
        grid=(B, nb),
        in_specs=[
            pl.BlockSpec((rb, width), lambda b, i: (b * nb + i, 0)),
            pl.BlockSpec((rb, width), lambda b, i: (b * nb + i, 1)),
            pl.BlockSpec((rb, width), lambda b, i: (b * nb + i, 2)),
            pl.BlockSpec((rb, width), lambda b, i: (b * nb + i, 3)),
            pl.BlockSpec((rb, LANES), lambda b, i: (b * nb + i, 0)),
            pl.BlockSpec((8, LANES), lambda b, i: (0, 0)),
            pl.BlockSpec((1, DV_C), lambda b, i: (0, 0)),
        ],
        out_specs=pl.BlockSpec((rb, width), lambda b, i: (b * nb + i, 0)),
        out_shape=jax.ShapeDtypeStruct((T, width), BF16),
        scratch_shapes=[pltpu.VMEM((H_C, DK_C, DV_C), F32)],
        compiler_params=_cparams(("parallel", "arbitrary")),
        name="gdn",
    )(hp, hp, hp, hp, tail, head_par, norm_g)


N_POST_PARAMS = 15


def _post_kernel(*refs, n_in, tm, tiles_per_seq, emit_bf16):
    ins = refs[:n_in]
    (wmix_ref, res_ref, g1_ref, b1_ref, kv_ref, wq_ref, wo_ref, g2_ref, b2_ref,
     win_ref, cw_ref, cb_ref, wout_ref, g3_ref, b3_ref) = refs[n_in:n_in + N_POST_PARAMS]
    n_out = 2 if emit_bf16 else 1
    outs = refs[n_in + N_POST_PARAMS:n_in + N_POST_PARAMS + n_out]
    att_sc, xcat_sc, hid_sc, act_sc = refs[n_in + N_POST_PARAMS + n_out:]

    y = ALPHA * res_ref[...]
    off = 0
    for r in ins:
        kdim = r.shape[1]
        y = y + _dot(r[...], wmix_ref[off:off + kdim, :])
        off += kdim
    h1 = _layer_norm(y, g1_ref[...], b1_ref[...])

    q = (_dot(h1.astype(BF16), wq_ref[...]) * (D_XA ** -0.5)).astype(BF16)
    for hd in range(N_XA):
        cs = slice(hd * D_XA, (hd + 1) * D_XA)
        s = _dot_nt(q[:, cs], kv_ref[:, cs])
        p = jnp.exp(s - jnp.max(s, axis=-1, keepdims=True))
        l = jnp.sum(p, axis=-1, keepdims=True)
        vs = slice(D_MODEL + hd * D_XA, D_MODEL + (hd + 1) * D_XA)
        att_sc[:, cs] = (_dot(p.astype(BF16), kv_ref[:, vs]) / l).astype(BF16)
    h2 = _layer_norm(ALPHA * h1 + _dot(att_sc[...], wo_ref[...]), g2_ref[...], b2_ref[...])

    is_start = (pl.program_id(0) % tiles_per_seq) == 0

    @pl.when(is_start)
    def _():
        xcat_sc[0:HALO, :] = jnp.zeros((HALO, D_MODEL), BF16)

    @pl.when(jnp.logical_not(is_start))
    def _():
        xcat_sc[0:HALO, :] = xcat_sc[tm:tm + HALO, :]

    xcat_sc[HALO:HALO + tm, :] = h2.astype(BF16)
    xc = xcat_sc[...]

    def conv(col):
        hid_sc[...] = _dot(xc, win_ref[:, col:col + FF_CHUNK])
        w = cw_ref[:, col:col + FF_CHUNK]
        y = cb_ref[:, col:col + FF_CHUNK] + w[FFN_CONV - 1:FFN_CONV, :] * hid_sc[HALO:HALO + tm, :]
        for j in range(FFN_CONV - 1):
            o = HALO - (FFN_CONV - 1) + j
            y = y + w[j:j + 1, :] * hid_sc[o:o + tm, :]
        return y

    for c in range(N_FF_CHUNKS):
        gate = conv(c * FF_CHUNK)
        up = conv(D_FF + c * FF_CHUNK)
        act_sc[:, c * FF_CHUNK:(c + 1) * FF_CHUNK] = (gate * _sigmoid(gate) * up).astype(BF16)
    out = _layer_norm(ALPHA * h2 + _dot(act_sc[...], wout_ref[...]), g3_ref[...], b3_ref[...])
    outs[0][...] = out
    if emit_bf16:
        outs[1][...] = out.astype(BF16)


def _post(ins, res, kv, params, *, S, tm, emit_bf16):
    (w_mix, g1, b1, wq, wo, g2, b2, w_in, cw, cb, w_out, g3, b3) = params
    T = res.shape[0]
    tiles_per_seq = S // tm
    kern = functools.partial(_post_kernel, n_in=len(ins), tm=tm, tiles_per_seq=tiles_per_seq, emit_bf16=emit_bf16)
    row_tile = pl.BlockSpec((tm, D_MODEL), lambda i: (i, 0))
    vec = pl.BlockSpec((1, D_MODEL), lambda i: (0, 0))
    in_specs = [pl.BlockSpec((tm, a.shape[1]), lambda i: (i, 0)) for a in ins]
    in_specs += [
        _const_spec(w_mix.shape), row_tile, vec, vec,
        pl.BlockSpec((N_MEM, 2 * D_MODEL), lambda i: (i // tiles_per_seq, 0)),
        _const_spec(wq.shape), _const_spec(wo.shape), vec, vec,
        _const_spec(w_in.shape), _const_spec(cw.shape), _const_spec(cb.shape), _const_spec(w_out.shape), vec, vec,
    ]
    out_specs = [row_tile]
    out_shape = [jax.ShapeDtypeStruct((T, D_MODEL), F32)]
    if emit_bf16:
        out_specs.append(row_tile)
        out_shape.append(jax.ShapeDtypeStruct((T, D_MODEL), BF16))
    return pl.pallas_call(
        kern,
        grid=(T // tm,),
        in_specs=in_specs,
        out_specs=out_specs,
        out_shape=out_shape,
        scratch_shapes=[
            pltpu.VMEM((tm, D_MODEL), BF16),
            pltpu.VMEM((tm + HALO, D_MODEL), BF16),
            pltpu.VMEM((tm + HALO, FF_CHUNK), F32),
            pltpu.VMEM((tm, D_FF), BF16),
        ],
        compiler_params=_cparams(("arbitrary",)),
        name="post",
    )(*ins, w_mix, res, g1, b1, kv, wq, wo, g2, b2, w_in, cw, cb, w_out, g3, b3)


def _rope_consts():
    inv_freq = ROPE_THETA ** (-jnp.arange(0, ROPE_DIM, 2, dtype=F32) / ROPE_DIM)
    d = jnp.arange(LANES) % D_A
    half = ROPE_DIM // 2
    freq = jnp.where(d < ROPE_DIM, inv_freq[d % half], 0.0)
    lo = jnp.where(d < half, -1.0, 0.0)
    hi = jnp.where((d >= half) & (d < ROPE_DIM), 1.0, 0.0)
    return jnp.zeros((8, LANES), F32).at[0].set(freq).at[1].set(lo).at[2].set(hi)


def _row(v):
    return v.reshape(1, -1).astype(F32)


def _tail_w(w, n_main):
    tail = w[:, n_main:]
    return jnp.pad(tail, ((0, 0), (0, LANES - tail.shape[1]))).astype(BF16)


def _post_layer(ins, res, mem2d, S, params, tm, emit_bf16):
    (w_mix, ln1_g, ln1_b, xa_wq, xa_wkv, xa_wo, ln2_g, ln2_b, f_in, f_cw, f_cb, f_out, ln3_g, ln3_b) = params
    kv = _kvproj(mem2d, xa_wkv.astype(BF16), tm=tm)
    prepared = (w_mix.astype(BF16), _row(ln1_g), _row(ln1_b), xa_wq.astype(BF16), xa_wo.astype(BF16),
                _row(ln2_g), _row(ln2_b), f_in.astype(BF16), f_cw.astype(F32), _row(f_cb), f_out.astype(BF16),
                _row(ln3_g), _row(ln3_b))
    return _post(ins, res, kv, prepared, S=S, tm=tm, emit_bf16=emit_bf16)


def kernel(x, mem, positions, w_in_0, diff_lambda_0, diff_norm_0, gla_w2_0, gla_b2_0, gla_norm_0, w_mix_out_0, ln1_g_0, ln1_b_0, xa_wq_0, xa_wkv_0, xa_wo_0, ln2_g_0, ln2_b_0, ffn_w_in_0, ffn_conv_w_0, ffn_conv_b_0, ffn_w_out_0, ln3_g_0, ln3_b_0, w_in_1, gdn_conv_w_1, gdn_a_log_1, gdn_dt_bias_1, gdn_norm_1, w_mix_out_1, ln1_g_1, ln1_b_1, xa_wq_1, xa_wkv_1, xa_wo_1, ln2_g_1, ln2_b_1, ffn_w_in_1, ffn_conv_w_1, ffn_conv_b_1, ffn_w_out_1, ln3_g_1, ln3_b_1):
    B, S, _ = x.shape
    T = B * S
    tm = min(512, S)
    x2d = x.reshape(T, D_MODEL)
    mem2d = mem.reshape(B * N_MEM, D_MODEL)

    hp, tail = _proj0(x2d, positions.reshape(T, 1), _rope_consts(),
                      w_in_0[:, :P0_MAIN].astype(BF16), _tail_w(w_in_0, P0_MAIN), tm=tm)
    lam_init = 0.8 - 0.6 * math.exp(-0.3 * 0)
    oa = _diffattn(hp, diff_lambda_0.astype(F32), diff_norm_0.reshape(DV_A, 1).astype(F32), B=B, S=S,
                   tq=min(256, S), lam_init=lam_init)
    w2p = jnp.pad(gla_w2_0, ((0, LANES - GATE_RANK), (0, 0))).astype(BF16)
    ob = _gla(hp, tail, w2p, _row(gla_b2_0), _row(gla_norm_0), B=B, S=S, rb=min(512, S))
    h, hb = _post_layer([oa, ob], x2d, mem2d, S,
                        (w_mix_out_0, ln1_g_0, ln1_b_0, xa_wq_0, xa_wkv_0, xa_wo_0, ln2_g_0, ln2_b_0,
                         ffn_w_in_0, ffn_conv_w_0, ffn_conv_b_0, ffn_w_out_0, ln3_g_0, ln3_b_0), tm, True)

    hp, tail = _proj1(hb, w_in_1[:, :P1_MAIN].astype(BF16), _tail_w(w_in_1, P1_MAIN), gdn_conv_w_1.astype(F32),
                      S=S, tm=tm)
    head_par = (jnp.zeros((8, LANES), F32)
                .at[0, H_C:2 * H_C].set(gdn_a_log_1.astype(F32))
                .at[1, H_C:2 * H_C].set(gdn_dt_bias_1.astype(F32)))
    oc = _gdn(hp, tail, head_par, _row(gdn_norm_1), B=B, S=S, rb=min(512, S))
    (out,) = _post_layer([oc], h, mem2d, S,
                         (w_mix_out_1, ln1_g_1, ln1_b_1, xa_wq_1, xa_wkv_1, xa_wo_1, ln2_g_1, ln2_b_1,
                          ffn_w_in_1, ffn_conv_w_1, ffn_conv_b_1, ffn_w_out_1, ln3_g_1, ln3_b_1), tm, False)
    return out.reshape(B, S, D_MODEL)
```

```python
import functools
import math

import jax
import jax.numpy as jnp
from jax import lax
from jax.experimental import pallas as pl
from jax.experimental.pallas import tpu as pltpu

F32 = jnp.float32
BF16 = jnp.bfloat16

D_MODEL = 1024
N_MEM = 256
H_A, D_A, DV_A = 4, 64, 128
ROPE_THETA, ROPE_DIM = 500000.0, 16
H_B, DK_B, DV_B = 4, 64, 128
GATE_RANK, GATE_NORMALIZER = 16, 16.0
H_C, DK_C, DV_C, CONV_C = 8, 128, 128, 4
CHUNK = 64
N_XA, D_XA = 4, 256
D_FF = 2816
FFN_CONV = 3
LN_EPS, RMS_EPS = 1e-5, 1e-6
DEPTH = 2
ALPHA = (2.0 * DEPTH) ** 0.25
P0_MAIN = 3072
P1_MAIN = 4096
LANES = 128
FF_CHUNK = 256
N_FF_CHUNKS = D_FF // FF_CHUNK
HALO = 16
VMEM_LIMIT = 56 * 1024 * 1024

NT_DIMS = (((1,), (1,)), ((), ()))
TN_DIMS = (((0,), (0,)), ((), ()))
HI = lax.Precision.HIGHEST


def _cparams(sem):
    return pltpu.CompilerParams(dimension_semantics=sem, vmem_limit_bytes=VMEM_LIMIT)


def _const_spec(shape):
    nd = len(shape)
    return pl.BlockSpec(shape, lambda *_: (0,) * nd, pipeline_mode=pl.Buffered(1))


def _dot(a, b):
    return jnp.dot(a, b, preferred_element_type=F32)


def _dot_nt(a, b, precision=None):
    return lax.dot_general(a, b, NT_DIMS, precision=precision, preferred_element_type=F32)


def _layer_norm(y, g, b):
    mu = jnp.mean(y, axis=-1, keepdims=True)
    yc = y - mu
    var = jnp.mean(yc * yc, axis=-1, keepdims=True)
    return yc * lax.rsqrt(var + LN_EPS) * g + b


def _sigmoid(x):
    return 1.0 / (1.0 + jnp.exp(-x))


def _softplus(x):
    return jnp.maximum(x, 0.0) + jnp.log(1.0 + jnp.exp(-jnp.abs(x)))


def _tril_mask(n, strict=False):
    r = lax.broadcasted_iota(jnp.int32, (n, n), 0)
    c = lax.broadcasted_iota(jnp.int32, (n, n), 1)
    return (c < r) if strict else (c <= r)


def _proj0_kernel(x_ref, pos_ref, rc_ref, w_ref, wt_ref, o_ref, t_ref):
    xb = x_ref[...].astype(BF16)
    ang = pos_ref[...].astype(F32) * rc_ref[0:1, :]
    c = jnp.cos(ang)
    s = jnp.sin(ang)
    s_lo = s * rc_ref[1:2, :]
    s_hi = s * rc_ref[2:3, :]
    q_scale = D_A ** -0.5
    for j in range(2):
        y = _dot(xb, w_ref[:, j * 512:(j + 1) * 512])
        for t in range(4):
            yt = y[:, t * LANES:(t + 1) * LANES]
            yt = yt * c + pltpu.roll(yt, LANES - 8, 1) * s_lo + pltpu.roll(yt, 8, 1) * s_hi
            if j == 0:
                yt = yt * q_scale
            col = j * 512 + t * LANES
            o_ref[:, col:col + LANES] = yt.astype(BF16)
    for col in range(1024, P0_MAIN, 512):
        o_ref[:, col:col + 512] = _dot(xb, w_ref[:, col:col + 512]).astype(BF16)
    t_ref[...] = _dot(xb, wt_ref[...])


def _proj0(x2d, pos2d, rope_c, w_main, w_tail, *, tm):
    T = x2d.shape[0]
    return pl.pallas_call(
        _proj0_kernel,
        grid=(T // tm,),
        in_specs=[
            pl.BlockSpec((tm, D_MODEL), lambda i: (i, 0)),
            pl.BlockSpec((tm, 1), lambda i: (i, 0)),
            _const_spec((8, LANES)),
            _const_spec((D_MODEL, P0_MAIN)),
            _const_spec((D_MODEL, LANES)),
        ],
        out_specs=[
            pl.BlockSpec((tm, P0_MAIN), lambda i: (i, 0)),
            pl.BlockSpec((tm, LANES), lambda i: (i, 0)),
        ],
        out_shape=[
            jax.ShapeDtypeStruct((T, P0_MAIN), BF16),
            jax.ShapeDtypeStruct((T, LANES), F32),
        ],
        compiler_params=_cparams(("parallel",)),
        name="proj0",
    )(x2d, pos2d, rope_c, w_main, w_tail)


QKV_C = 3 * H_C * DK_C
PROJ_CHUNK = 512


def _proj1_kernel(x_ref, w_ref, wt_ref, cw_ref, o_ref, t_ref, ybuf_sc, carry_sc, *, tm, tiles_per_seq):
    xb = x_ref[...]

    @pl.when((pl.program_id(0) % tiles_per_seq) == 0)
    def _():
        carry_sc[...] = jnp.zeros(carry_sc.shape, F32)

    def plain(col):
        o_ref[:, col:col + PROJ_CHUNK] = _dot(xb, w_ref[:, col:col + PROJ_CHUNK]).astype(BF16)

    def tail():
        t_ref[...] = _dot(xb, wt_ref[...])

    fillers = [functools.partial(plain, col) for col in range(QKV_C, P1_MAIN, PROJ_CHUNK)] + [tail]
    for ci, col in enumerate(range(0, QKV_C, PROJ_CHUNK)):
        ybuf_sc[0:8, :] = carry_sc[ci]
        ybuf_sc[8:8 + tm, :] = _dot(xb, w_ref[:, col:col + PROJ_CHUNK])
        carry_sc[ci] = ybuf_sc[tm:tm + 8, :]
        if ci % 2 == 0 and fillers:
            fillers.pop(0)()
        w = cw_ref[:, col:col + PROJ_CHUNK]
        y = w[CONV_C - 1:CONV_C, :] * ybuf_sc[8:8 + tm, :]
        for j in range(CONV_C - 1):
            y = y + w[j:j + 1, :] * ybuf_sc[5 + j:5 + j + tm, :]
        y = y * _sigmoid(y)
        if col < 2 * H_C * DK_C:
            scale = DK_C ** -0.5 if col < H_C * DK_C else 1.0
            for t in range(PROJ_CHUNK // LANES):
                yt = y[:, t * LANES:(t + 1) * LANES]
                yt = yt * (lax.rsqrt(jnp.sum(yt * yt, axis=-1, keepdims=True) + RMS_EPS) * scale)
                o_ref[:, col + t * LANES:col + (t + 1) * LANES] = yt.astype(BF16)
        else:
            o_ref[:, col:col + PROJ_CHUNK] = y.astype(BF16)
    for f in fillers:
        f()


def _proj1(xb2d, w_main, w_tail, conv_w, *, S, tm):
    T = xb2d.shape[0]
    kern = functools.partial(_proj1_kernel, tm=tm, tiles_per_seq=S // tm)
    return pl.pallas_call(
        kern,
        grid=(T // tm,),
        in_specs=[
            pl.BlockSpec((tm, D_MODEL), lambda i: (i, 0)),
            _const_spec((D_MODEL, P1_MAIN)),
            _const_spec((D_MODEL, LANES)),
            _const_spec((CONV_C, QKV_C)),
        ],
        out_specs=[
            pl.BlockSpec((tm, P1_MAIN), lambda i: (i, 0)),
            pl.BlockSpec((tm, LANES), lambda i: (i, 0)),
        ],
        out_shape=[
            jax.ShapeDtypeStruct((T, P1_MAIN), BF16),
            jax.ShapeDtypeStruct((T, LANES), F32),
        ],
        scratch_shapes=[
            pltpu.VMEM((tm + 8, PROJ_CHUNK), F32),
            pltpu.VMEM((QKV_C // PROJ_CHUNK, 8, PROJ_CHUNK), F32),
        ],
        compiler_params=_cparams(("arbitrary",)),
        name="proj1",
    )(xb2d, w_main, w_tail, conv_w)


def _kvproj_kernel(m_ref, w_ref, o_ref):
    mb = m_ref[...].astype(BF16)
    for col in range(0, 2 * D_MODEL, 512):
        o_ref[:, col:col + 512] = _dot(mb, w_ref[:, col:col + 512]).astype(BF16)


def _kvproj(mem2d, wkv, *, tm):
    R = mem2d.shape[0]
    return pl.pallas_call(
        _kvproj_kernel,
        grid=(R // tm,),
        in_specs=[
            pl.BlockSpec((tm, D_MODEL), lambda i: (i, 0)),
            _const_spec((D_MODEL, 2 * D_MODEL)),
        ],
        out_specs=pl.BlockSpec((tm, 2 * D_MODEL), lambda i: (i, 0)),
        out_shape=jax.ShapeDtypeStruct((R, 2 * D_MODEL), BF16),
        compiler_params=_cparams(("parallel",)),
        name="kvproj",
    )(mem2d, wkv)


def _diffattn_kernel(q_ref, k_ref, v_ref, lam_ref, g_ref, o_ref, *, S, tq, lam_init):
    lv = lam_ref[...]
    lam = (jnp.exp(jnp.sum(lv[0:1] * lv[1:2], axis=-1, keepdims=True))
           - jnp.exp(jnp.sum(lv[2:3] * lv[3:4], axis=-1, keepdims=True)) + lam_init)
    lane = lax.broadcasted_iota(jnp.int32, (tq, LANES), 1)
    r = lax.broadcasted_iota(jnp.int32, (2 * tq, tq), 0)
    c = lax.broadcasted_iota(jnp.int32, (2 * tq, tq), 1)
    causal = c <= jnp.where(r >= tq, r - tq, r)
    vt = v_ref[...].astype(F32).T
    vt_ext = jnp.concatenate([vt, jnp.ones((HALO, S), F32)], axis=0).astype(BF16)
    g_col = g_ref[...]
    def scores(qi):
        n_keys = (qi + 1) * tq
        q = q_ref[qi * tq:(qi + 1) * tq, :]
        zero = jnp.zeros_like(q)
        qq = jnp.concatenate([jnp.where(lane < D_A, q, zero), jnp.where(lane >= D_A, q, zero)], axis=0)
        s = _dot_nt(qq, k_ref[0:n_keys, :])
        s_diag = jnp.where(causal, s[:, n_keys - tq:], -jnp.inf)
        return s_diag if qi == 0 else jnp.concatenate([s[:, :n_keys - tq], s_diag], axis=1)

    def softmax_num(s):
        return jnp.exp(s - jnp.max(s, axis=-1, keepdims=True)).astype(BF16)

    def finish(qi, p):
        n_keys = (qi + 1) * tq
        ot = _dot_nt(vt_ext[:, 0:n_keys], p)
        o_all = ot[:DV_A] / ot[DV_A:DV_A + 1]
        o = o_all[:, :tq] - lam * o_all[:, tq:]
        ms = jnp.mean(o * o, axis=0, keepdims=True)
        y = o * lax.rsqrt(ms + RMS_EPS) * g_col * (1.0 - lam_init)
        o_ref[qi * tq:(qi + 1) * tq, :] = y.T.astype(BF16)

    nq = S // tq
    pairs = [(i, nq - 1 - i) for i in range(nq // 2)] + ([(nq // 2,)] if nq % 2 else [])
    order = [sum(pairs[i:i + 2], ()) for i in range(0, len(pairs), 2)]
    for group in order:
        ss = [scores(qi) for qi in group]
        ps = [softmax_num(s) for s in ss]
        for qi, p in zip(group, ps):
            finish(qi, p)


def _diffattn(hp, lam_vecs, norm_g, *, B, S, tq, lam_init):
    T = B * S
    kern = functools.partial(_diffattn_kernel, S=S, tq=tq, lam_init=lam_init)
    return pl.pallas_call(
        kern,
        grid=(B, H_A),
        in_specs=[
            pl.BlockSpec((S, LANES), lambda b, h: (b, h)),
            pl.BlockSpec((S, LANES), lambda b, h: (b, H_A + h)),
            pl.BlockSpec((S, LANES), lambda b, h: (b, 2 * H_A + h)),
            pl.BlockSpec((4, D_A), lambda b, h: (0, 0)),
            pl.BlockSpec((DV_A, 1), lambda b, h: (0, 0)),
        ],
        out_specs=pl.BlockSpec((S, LANES), lambda b, h: (b, h)),
        out_shape=jax.ShapeDtypeStruct((T, H_A * DV_A), BF16),
        compiler_params=_cparams(("parallel", "parallel")),
        name="diffattn",
    )(hp, hp, hp, lam_vecs, norm_g)


GLA_GROUP = 2 * CHUNK
GLA_LEVELS = (32, 16, 8, 4, 2, 1)


def _split_dot(m_bf16, hi, lo):
    return _dot(m_bf16, hi) + _dot(m_bf16, lo)


def _gla_kernel(q_ref, k_ref, v_ref, r_ref, t_ref, w2_ref, b2_ref, g_ref, o_ref, st_sc, *, rb):
    G = GLA_GROUP

    @pl.when(pl.program_id(1) == 0)
    def _():
        st_sc[...] = jnp.zeros(st_sc.shape, F32)

    r = lax.broadcasted_iota(jnp.int32, (G, G), 0)
    c = lax.broadcasted_iota(jnp.int32, (G, G), 1)
    tril2 = jnp.where(((r >= CHUNK) == (c >= CHUNK)) & (c <= r), 1.0, 0.0).astype(BF16)
    row = lax.broadcasted_iota(jnp.int32, (G, H_B * DK_B), 0)
    lane = lax.broadcasted_iota(jnp.int32, (G, LANES), 1)
    head_lanes = [lane < DK_B, lane >= DK_B]
    lane_c = lax.broadcasted_iota(jnp.int32, (CHUNK, LANES), 1)
    head_lanes_c = [lane_c < DK_B, lane_c >= DK_B]
    levels = []
    for w in GLA_LEVELS:
        same = (r // (2 * w)) == (c // (2 * w))
        r_in_i = (r % (2 * w)) >= w
        c_in_i = (c % (2 * w)) >= w
        in_range = (r_in_i & c_in_i & (c <= r)) | ((~r_in_i) & (~c_in_i) & (c > r))
        m_arg = jnp.where(same & in_range, 1.0, 0.0).astype(BF16)
        levels.append((m_arg, same, (row % (2 * w)) >= w))
    eye = r == c

    def group(gi, carry):
        r0 = pl.multiple_of(gi * G, G)
        rows = pl.ds(r0, G)
        x = _dot(t_ref[rows, :].astype(BF16), w2_ref[...]) + b2_ref[...]
        gk = (jnp.minimum(x, 0.0) - jnp.log(1.0 + jnp.exp(-jnp.abs(x)))) * (1.0 / GATE_NORMALIZER)
        hi = gk.astype(BF16)
        lo = (gk - hi.astype(F32)).astype(BF16)
        b = _split_dot(tril2, hi, lo)
        q = q_ref[rows, :].astype(F32) * (DK_B ** -0.5)
        k = k_ref[rows, :].astype(F32)
        v = v_ref[rows, :]
        q_lv, k_lv, masks = [], [], []
        for m_arg, same, row_in_i in levels:
            e = jnp.exp(_split_dot(m_arg, hi, lo))
            q_lv.append(jnp.where(row_in_i, q * e, 0.0))
            k_lv.append(jnp.where(row_in_i, 0.0, k * e).astype(BF16))
            masks.append(same)
        q_lv.append(q)
        k_lv.append(k.astype(BF16))
        masks.append(eye)
        atts = [None] * H_B
        for ql, kl_, mask in zip(q_lv, k_lv, masks):
            for h in range(H_B):
                blk = slice((h // 2) * LANES, (h // 2 + 1) * LANES)
                qm = jnp.where(head_lanes[h % 2], ql[:, blk], 0.0).astype(BF16)
                part = jnp.where(mask, _dot_nt(qm, kl_[:, blk]), 0.0)
                atts[h] = part if atts[h] is None else atts[h] + part
        avs = [_dot(atts[h].astype(BF16), v[:, h * DV_B:(h + 1) * DV_B]) for h in range(H_B)]
        qe = q * jnp.exp(b)
        sts = [st_sc[p] for p in range(H_B // 2)]
        inters = [[] for _ in range(H_B)]
        for c2 in range(G // CHUNK):
            cs = slice(c2 * CHUNK, (c2 + 1) * CHUNK)
            for h in range(H_B):
                blk = slice((h // 2) * LANES, (h // 2 + 1) * LANES)
                qm = jnp.where(head_lanes_c[h % 2], qe[cs, blk], 0.0).astype(BF16)
                inters[h].append(_dot_nt(qm, sts[h // 2].astype(BF16)))
            bl = b[(c2 + 1) * CHUNK - 1:(c2 + 1) * CHUNK, :]
            kl = (k[cs] * jnp.exp(bl - b[cs])).astype(BF16)
            ebl = jnp.exp(bl)
            kvs = [lax.dot_general(v[cs, h * DV_B:(h + 1) * DV_B], kl[:, (h // 2) * LANES:(h // 2 + 1) * LANES],
                                   TN_DIMS, preferred_element_type=F32) for h in range(H_B)]
            sts = [sts[p] * ebl[:, p * LANES:(p + 1) * LANES]
                   + jnp.where(head_lanes[0], kvs[2 * p], kvs[2 * p + 1]) for p in range(H_B // 2)]
        for p in range(H_B // 2):
            st_sc[p] = sts[p]
        for h in range(H_B):
            vs = slice(h * DV_B, (h + 1) * DV_B)
            o = jnp.concatenate(inters[h], axis=0) + avs[h]
            ms = jnp.mean(o * o, axis=-1, keepdims=True)
            gate = r_ref[rows, vs].astype(F32)
            y = o * lax.rsqrt(ms + RMS_EPS) * g_ref[...] * (gate * _sigmoid(gate))
            o_ref[rows, vs] = y.astype(BF16)
        return carry

    lax.fori_loop(0, rb // G, group, 0, unroll=4)


def _gla(hp, tail, w2p, b2, norm_g, *, B, S, rb):
    nb = S // rb
    T = B * S
    kern = functools.partial(_gla_kernel, rb=rb)
    return pl.pallas_call(
        kern,
        grid=(B, nb),
        in_specs=[
            pl.BlockSpec((rb, H_B * DK_B), lambda b, i: (b * nb + i, 6)),
            pl.BlockSpec((rb, H_B * DK_B), lambda b, i: (b * nb + i, 7)),
            pl.BlockSpec((rb, H_B * DV_B), lambda b, i: (b * nb + i, 4)),
            pl.BlockSpec((rb, H_B * DV_B), lambda b, i: (b * nb + i, 5)),
            pl.BlockSpec((rb, LANES), lambda b, i: (b * nb + i, 0)),
            pl.BlockSpec((LANES, H_B * DK_B), lambda b, i: (0, 0)),
            pl.BlockSpec((1, H_B * DK_B), lambda b, i: (0, 0)),
            pl.BlockSpec((1, DV_B), lambda b, i: (0, 0)),
        ],
        out_specs=pl.BlockSpec((rb, H_B * DV_B), lambda b, i: (b * nb + i, 0)),
        out_shape=jax.ShapeDtypeStruct((T, H_B * DV_B), BF16),
        scratch_shapes=[pltpu.VMEM((H_B // 2, DV_B, 2 * DK_B), F32)],
        compiler_params=_cparams(("parallel", "arbitrary")),
        name="gla",
    )(hp, hp, hp, hp, tail, w2p, b2, norm_g)


GDN_GROUP = 2 * CHUNK


def _gdn_kernel(q_ref, k_ref, v_ref, z_ref, t_ref, hp_ref, g_ref, o_ref, s_sc, *, rb):
    G = GDN_GROUP

    @pl.when(pl.program_id(1) == 0)
    def _():
        s_sc[...] = jnp.zeros(s_sc.shape, F32)

    r = lax.broadcasted_iota(jnp.int32, (G, G), 0)
    c = lax.broadcasted_iota(jnp.int32, (G, G), 1)
    same = (r >= CHUNK) == (c >= CHUNK)
    incl = same & (c <= r)
    strict = same & (c < r)
    tril2 = jnp.where(incl, 1.0, 0.0).astype(BF16)
    eye = (r == c).astype(F32)

    def group(gi, carry):
        r0 = pl.multiple_of(gi * G, G)
        rows = pl.ds(r0, G)
        t = t_ref[rows, :]
        beta_all = _sigmoid(t)
        g_all = -jnp.exp(hp_ref[0:1, :]) * _softplus(t + hp_ref[1:2, :])
        g_hi = g_all.astype(BF16)
        g_rest = g_all - g_hi.astype(F32)
        g_mid = g_rest.astype(BF16)
        g_lo = (g_rest - g_mid.astype(F32)).astype(BF16)
        cum_all = _dot(tril2, g_hi) + _dot(tril2, g_mid) + _dot(tril2, g_lo)
        cum_all_t = cum_all.T
        heads = range(H_C)
        khbs = [k_ref[rows, h * LANES:(h + 1) * LANES] for h in heads]
        qhbs = [q_ref[rows, h * LANES:(h + 1) * LANES] for h in heads]
        qs = [q.astype(F32) for q in qhbs]
        ks = [k.astype(F32) for k in khbs]
        vs = [v_ref[rows, h * LANES:(h + 1) * LANES].astype(F32) for h in heads]
        bhs = [jnp.broadcast_to(beta_all[:, h:h + 1], (G, LANES)) for h in heads]
        chs = [jnp.broadcast_to(cum_all[:, H_C + h:H_C + h + 1], (G, LANES)) for h in heads]
        crows = [jnp.broadcast_to(cum_all_t[H_C + h:H_C + h + 1, :], (G, G)) for h in heads]
        decs = [jnp.where(incl, jnp.exp(ch - crow), 0.0) for ch, crow in zip(chs, crows)]
        egs = [jnp.exp(ch) for ch in chs]
        kbs = [k * bh for k, bh in zip(ks, bhs)]
        kks = [_dot_nt(kb.astype(BF16), khb) for kb, khb in zip(kbs, khbs)]
        ps = [-jnp.where(strict, kk * dec, 0.0) for kk, dec in zip(kks, decs)]
        tinvs = [eye + p for p in ps]
        for _ in range(5):
            pbs = [p.astype(BF16) for p in ps]
            ps = [_dot(pb, pb) for pb in pbs]
            tinvs = [tinv + _dot(p.astype(BF16), tinv.astype(BF16)) for p, tinv in zip(ps, tinvs)]
        rhss = [jnp.concatenate([v * bh, kb * eg], axis=1).astype(BF16)
                for v, bh, kb, eg in zip(vs, bhs, kbs, egs)]
        uws = [_dot(tinv.astype(BF16), rhs) for tinv, rhs in zip(tinvs, rhss)]
        qks = [_dot_nt(qb, khb) * dec for qb, khb, dec in zip(qhbs, khbs, decs)]
        qgs = [q * eg for q, eg in zip(qs, egs)]
        sts = [s_sc[h] for h in heads]
        outs = [[] for _ in heads]
        for c2 in range(G // CHUNK):
            cs = slice(c2 * CHUNK, (c2 + 1) * CHUNK)
            wss = [_dot(jnp.concatenate([uw[cs, DV_C:], qg[cs]], axis=0).astype(BF16), st.astype(BF16))
                   for uw, qg, st in zip(uws, qgs, sts)]
            v_news = [uw[cs, :DV_C] - ws[:CHUNK] for uw, ws in zip(uws, wss)]
            for h in heads:
                outs[h].append(wss[h][CHUNK:] + _dot(qks[h][cs, cs].astype(BF16), v_news[h].astype(BF16)))
            gls = [ch[(c2 + 1) * CHUNK - 1:(c2 + 1) * CHUNK, :] for ch in chs]
            kgs = [k[cs] * jnp.exp(gl - ch[cs]) for k, gl, ch in zip(ks, gls, chs)]
            sts = [st * jnp.exp(gl) + _dot(kg.T.astype(BF16), v_new.astype(BF16))
                   for st, gl, kg, v_new in zip(sts, gls, kgs, v_news)]
        for h in heads:
            ls = slice(h * LANES, (h + 1) * LANES)
            s_sc[h] = sts[h]
            o = jnp.concatenate(outs[h], axis=0)
            ms = jnp.mean(o * o, axis=-1, keepdims=True)
            z = z_ref[pl.ds(r0, G), ls].astype(F32)
            y = o * lax.rsqrt(ms + RMS_EPS) * g_ref[...] * (z * _sigmoid(z))
            o_ref[pl.ds(r0, G), ls] = y.astype(BF16)
        return carry

    lax.fori_loop(0, rb // G, group, 0, unroll=2)


def _gdn(hp, tail, head_par, norm_g, *, B, S, rb):
    nb = S // rb
    T = B * S
    width = H_C * DK_C
    kern = functools.partial(_gdn_kernel, rb=rb)
    return pl.pallas_call(
        kern,
        grid=(B, nb),
        in_specs=[
            pl.BlockSpec((rb, width), lambda b, i: (b * nb + i, 0)),
            pl.BlockSpec((rb, width), lambda b, i: (b * nb + i, 1)),
            pl.BlockSpec((rb, width), lambda b, i: (b * nb + i, 2)),
            pl.BlockSpec((rb, width), lambda b, i: (b * nb + i, 3)),
            pl.BlockSpec((rb, LANES), lambda b, i: (b * nb + i, 0)),
            pl.BlockSpec((8, LANES), lambda b, i: (0, 0)),
            pl.BlockSpec((1, DV_C), lambda b, i: (0, 0)),
        ],
        out_specs=pl.BlockSpec((rb, width), lambda b, i: (b * nb + i, 0)),
        out_shape=jax.ShapeDtypeStruct((T, width), BF16),
        scratch_shapes=[pltpu.VMEM((H_C, DK_C, DV_C), F32)],
        compiler_params=_cparams(("parallel", "arbitrary")),
        name="gdn",
    )(hp, hp, hp, hp, tail, head_par, norm_g)


N_POST_PARAMS = 15


def _post_kernel(*refs, n_in, tm, tiles_per_seq, emit_bf16):
    ins = refs[:n_in]
    (wmix_ref, res_ref, g1_ref, b1_ref, kv_ref, wq_ref, wo_ref, g2_ref, b2_ref,
     win_ref, cw_ref, cb_ref, wout_ref, g3_ref, b3_ref) = refs[n_in:n_in + N_POST_PARAMS]
    n_out = 2 if emit_bf16 else 1
    outs = refs[n_in + N_POST_PARAMS:n_in + N_POST_PARAMS + n_out]
    att_sc, xcat_sc, hid_sc, act_sc = refs[n_in + N_POST_PARAMS + n_out:]

    y = ALPHA * res_ref[...]
    off = 0
    for r in ins:
        kdim = r.shape[1]
        y = y + _dot(r[...], wmix_ref[off:off + kdim, :])
        off += kdim
    h1 = _layer_norm(y, g1_ref[...], b1_ref[...])

    q = (_dot(h1.astype(BF16), wq_ref[...]) * (D_XA ** -0.5)).astype(BF16)
    for hd in range(N_XA):
        cs = slice(hd * D_XA, (hd + 1) * D_XA)
        s = _dot_nt(q[:, cs], kv_ref[:, cs])
        p = jnp.exp(s - jnp.max(s, axis=-1, keepdims=True))
        l = jnp.sum(p, axis=-1, keepdims=True)
        vs = slice(D_MODEL + hd * D_XA, D_MODEL + (hd + 1) * D_XA)
        att_sc[:, cs] = (_dot(p.astype(BF16), kv_ref[:, vs]) / l).astype(BF16)
    h2 = _layer_norm(ALPHA * h1 + _dot(att_sc[...], wo_ref[...]), g2_ref[...], b2_ref[...])

    is_start = (pl.program_id(0) % tiles_per_seq) == 0

    @pl.when(is_start)
    def _():
        xcat_sc[0:HALO, :] = jnp.zeros((HALO, D_MODEL), BF16)

    @pl.when(jnp.logical_not(is_start))
    def _():
        xcat_sc[0:HALO, :] = xcat_sc[tm:tm + HALO, :]

    xcat_sc[HALO:HALO + tm, :] = h2.astype(BF16)
    xc = xcat_sc[...]

    def conv(col):
        hid_sc[...] = _dot(xc, win_ref[:, col:col + FF_CHUNK])
        w = cw_ref[:, col:col + FF_CHUNK]
        y = cb_ref[:, col:col + FF_CHUNK] + w[FFN_CONV - 1:FFN_CONV, :] * hid_sc[HALO:HALO + tm, :]
        for j in range(FFN_CONV - 1):
            o = HALO - (FFN_CONV - 1) + j
            y = y + w[j:j + 1, :] * hid_sc[o:o + tm, :]
        return y

    for c in range(N_FF_CHUNKS):
        gate = conv(c * FF_CHUNK)
        up = conv(D_FF + c * FF_CHUNK)
        act_sc[:, c * FF_CHUNK:(c + 1) * FF_CHUNK] = (gate * _sigmoid(gate) * up).astype(BF16)
    out = _layer_norm(ALPHA * h2 + _dot(act_sc[...], wout_ref[...]), g3_ref[...], b3_ref[...])
    outs[0][...] = out
    if emit_bf16:
        outs[1][...] = out.astype(BF16)


def _post(ins, res, kv, params, *, S, tm, emit_bf16):
    (w_mix, g1, b1, wq, wo, g2, b2, w_in, cw, cb, w_out, g3, b3) = params
    T = res.shape[0]
    tiles_per_seq = S // tm
    kern = functools.partial(_post_kernel, n_in=len(ins), tm=tm, tiles_per_seq=tiles_per_seq, emit_bf16=emit_bf16)
    row_tile = pl.BlockSpec((tm, D_MODEL), lambda i: (i, 0))
    vec = pl.BlockSpec((1, D_MODEL), lambda i: (0, 0))
    in_specs = [pl.BlockSpec((tm, a.shape[1]), lambda i: (i, 0)) for a in ins]
    in_specs += [
        _const_spec(w_mix.shape), row_tile, vec, vec,
        pl.BlockSpec((N_MEM, 2 * D_MODEL), lambda i: (i // tiles_per_seq, 0)),
        _const_spec(wq.shape), _const_spec(wo.shape), vec, vec,
        _const_spec(w_in.shape), _const_spec(cw.shape), _const_spec(cb.shape), _const_spec(w_out.shape), vec, vec,
    ]
    out_specs = [row_tile]
    out_shape = [jax.ShapeDtypeStruct((T, D_MODEL), F32)]
    if emit_bf16:
        out_specs.append(row_tile)
        out_shape.append(jax.ShapeDtypeStruct((T, D_MODEL), BF16))
    return pl.pallas_call(
        kern,
        grid=(T // tm,),
        in_specs=in_specs,
        out_specs=out_specs,
        out_shape=out_shape,
        scratch_shapes=[
            pltpu.VMEM((tm, D_MODEL), BF16),
            pltpu.VMEM((tm + HALO, D_MODEL), BF16),
            pltpu.VMEM((tm + HALO, FF_CHUNK), F32),
            pltpu.VMEM((tm, D_FF), BF16),
        ],
        compiler_params=_cparams(("arbitrary",)),
        name="post",
    )(*ins, w_mix, res, g1, b1, kv, wq, wo, g2, b2, w_in, cw, cb, w_out, g3, b3)


def _rope_consts():
    inv_freq = ROPE_THETA ** (-jnp.arange(0, ROPE_DIM, 2, dtype=F32) / ROPE_DIM)
    d = jnp.arange(LANES) % D_A
    half = ROPE_DIM // 2
    freq = jnp.where(d < ROPE_DIM, inv_freq[d % half], 0.0)
    lo = jnp.where(d < half, -1.0, 0.0)
    hi = jnp.where((d >= half) & (d < ROPE_DIM), 1.0, 0.0)
    return jnp.zeros((8, LANES), F32).at[0].set(freq).at[1].set(lo).at[2].set(hi)


def _row(v):
    return v.reshape(1, -1).astype(F32)


def _tail_w(w, n_main):
    tail = w[:, n_main:]
    return jnp.pad(tail, ((0, 0), (0, LANES - tail.shape[1]))).astype(BF16)


def _post_layer(ins, res, mem2d, S, params, tm, emit_bf16):
    (w_mix, ln1_g, ln1_b, xa_wq, xa_wkv, xa_wo, ln2_g, ln2_b, f_in, f_cw, f_cb, f_out, ln3_g, ln3_b) = params
    kv = _kvproj(mem2d, xa_wkv.astype(BF16), tm=tm)
    prepared = (w_mix.astype(BF16), _row(ln1_g), _row(ln1_b), xa_wq.astype(BF16), xa_wo.astype(BF16),
                _row(ln2_g), _row(ln2_b), f_in.astype(BF16), f_cw.astype(F32), _row(f_cb), f_out.astype(BF16),
                _row(ln3_g), _row(ln3_b))
    return _post(ins, res, kv, prepared, S=S, tm=tm, emit_bf16=emit_bf16)


def kernel(x, mem, positions, w_in_0, diff_lambda_0, diff_norm_0, gla_w2_0, gla_b2_0, gla_norm_0, w_mix_out_0, ln1_g_0, ln1_b_0, xa_wq_0, xa_wkv_0, xa_wo_0, ln2_g_0, ln2_b_0, ffn_w_in_0, ffn_conv_w_0, ffn_conv_b_0, ffn_w_out_0, ln3_g_0, ln3_b_0, w_in_1, gdn_conv_w_1, gdn_a_log_1, gdn_dt_bias_1, gdn_norm_1, w_mix_out_1, ln1_g_1, ln1_b_1, xa_wq_1, xa_wkv_1, xa_wo_1, ln2_g_1, ln2_b_1, ffn_w_in_1, ffn_conv_w_1, ffn_conv_b_1, ffn_w_out_1, ln3_g_1, ln3_b_1):
    B, S, _ = x.shape
    T = B * S
    tm = min(512, S)
    x2d = x.reshape(T, D_MODEL)
    mem2d = mem.reshape(B * N_MEM, D_MODEL)

    hp, tail = _proj0(x2d, positions.reshape(T, 1), _rope_consts(),
                      w_in_0[:, :P0_MAIN].astype(BF16), _tail_w(w_in_0, P0_MAIN), tm=tm)
    lam_init = 0.8 - 0.6 * math.exp(-0.3 * 0)
    oa = _diffattn(hp, diff_lambda_0.astype(F32), diff_norm_0.reshape(DV_A, 1).astype(F32), B=B, S=S,
                   tq=min(256, S), lam_init=lam_init)
    w2p = jnp.pad(gla_w2_0, ((0, LANES - GATE_RANK), (0, 0))).astype(BF16)
    ob = _gla(hp, tail, w2p, _row(gla_b2_0), _row(gla_norm_0), B=B, S=S, rb=min(512, S))
    h, hb = _post_layer([oa, ob], x2d, mem2d, S,
                        (w_mix_out_0, ln1_g_0, ln1_b_0, xa_wq_0, xa_wkv_0, xa_wo_0, ln2_g_0, ln2_b_0,
                         ffn_w_in_0, ffn_conv_w_0, ffn_conv_b_0, ffn_w_out_0, ln3_g_0, ln3_b_0), tm, True)

    hp, tail = _proj1(hb, w_in_1[:, :P1_MAIN].astype(BF16), _tail_w(w_in_1, P1_MAIN), gdn_conv_w_1.astype(F32),
                      S=S, tm=tm)
    head_par = (jnp.zeros((8, LANES), F32)
                .at[0, H_C:2 * H_C].set(gdn_a_log_1.astype(F32))
                .at[1, H_C:2 * H_C].set(gdn_dt_bias_1.astype(F32)))
    oc = _gdn(hp, tail, head_par, _row(gdn_norm_1), B=B, S=S, rb=min(512, S))
    (out,) = _post_layer([oc], h, mem2d, S,
                         (w_mix_out_1, ln1_g_1, ln1_b_1, xa_wq_1, xa_wkv_1, xa_wo_1, ln2_g_1, ln2_b_1,
                          ffn_w_in_1, ffn_conv_w_1, ffn_conv_b_1, ffn_w_out_1, ln3_g_1, ln3_b_1), tm, False)
    return out.reshape(B, S, D_MODEL)
```

```python
import functools
import math

import jax
import jax.numpy as jnp
from jax import lax
from jax.experimental import pallas as pl
from jax.experimental.pallas import tpu as pltpu

F32 = jnp.float32
BF16 = jnp.bfloat16

D_MODEL = 1024
N_MEM = 256
H_A, D_A, DV_A = 4, 64, 128
ROPE_THETA, ROPE_DIM = 500000.0, 16
H_B, DK_B, DV_B = 4, 64, 128
GATE_RANK, GATE_NORMALIZER = 16, 16.0
H_C, DK_C, DV_C, CONV_C = 8, 128, 128, 4
CHUNK = 64
N_XA, D_XA = 4, 256
D_FF = 2816
FFN_CONV = 3
LN_EPS, RMS_EPS = 1e-5, 1e-6
DEPTH = 2
ALPHA = (2.0 * DEPTH) ** 0.25
P0_MAIN = 3072
P1_MAIN = 4096
LANES = 128
FF_CHUNK = 256
N_FF_CHUNKS = D_FF // FF_CHUNK
HALO = 16
VMEM_LIMIT = 56 * 1024 * 1024

NT_DIMS = (((1,), (1,)), ((), ()))
TN_DIMS = (((0,), (0,)), ((), ()))
HI = lax.Precision.HIGHEST


def _cparams(sem):
    return pltpu.CompilerParams(dimension_semantics=sem, vmem_limit_bytes=VMEM_LIMIT)


def _const_spec(shape):
    nd = len(shape)
    return pl.BlockSpec(shape, lambda *_: (0,) * nd, pipeline_mode=pl.Buffered(1))


def _dot(a, b):
    return jnp.dot(a, b, preferred_element_type=F32)


def _dot_nt(a, b, precision=None):
    return lax.dot_general(a, b, NT_DIMS, precision=precision, preferred_element_type=F32)


def _layer_norm(y, g, b):
    mu = jnp.mean(y, axis=-1, keepdims=True)
    yc = y - mu
    var = jnp.mean(yc * yc, axis=-1, keepdims=True)
    return yc * lax.rsqrt(var + LN_EPS) * g + b


def _sigmoid(x):
    return 1.0 / (1.0 + jnp.exp(-x))


def _softplus(x):
    return jnp.maximum(x, 0.0) + jnp.log(1.0 + jnp.exp(-jnp.abs(x)))


def _tril_mask(n, strict=False):
    r = lax.broadcasted_iota(jnp.int32, (n, n), 0)
    c = lax.broadcasted_iota(jnp.int32, (n, n), 1)
    return (c < r) if strict else (c <= r)


def _proj0_kernel(x_ref, pos_ref, freq_ref, expand_ref, rc_ref, w_ref, wt_ref, o_ref, t_ref):
    xb = x_ref[...].astype(BF16)
    ang_t = freq_ref[...] * pos_ref[0].astype(F32)
    trig_t = jnp.concatenate([jnp.cos(ang_t), jnp.sin(ang_t)], axis=0)
    hi = trig_t.astype(BF16)
    lo = (trig_t - hi.astype(F32)).astype(BF16)
    e = expand_ref[...]
    pat = (lax.dot_general(hi, e, TN_DIMS, preferred_element_type=F32)
           + lax.dot_general(lo, e, TN_DIMS, preferred_element_type=F32))
    c = pat[:, 0:LANES] + rc_ref[0:1, :]
    s_lo = pat[:, LANES:2 * LANES] * rc_ref[1:2, :]
    s_hi = pat[:, LANES:2 * LANES] * rc_ref[2:3, :]
    q_scale = D_A ** -0.5
    for j in range(2):
        y = _dot(xb, w_ref[:, j * 512:(j + 1) * 512])
        for t in range(4):
            yt = y[:, t * LANES:(t + 1) * LANES]
            yt = yt * c + pltpu.roll(yt, LANES - 8, 1) * s_lo + pltpu.roll(yt, 8, 1) * s_hi
            if j == 0:
                yt = yt * q_scale
            col = j * 512 + t * LANES
            o_ref[:, col:col + LANES] = yt.astype(BF16)
    for col in range(1024, P0_MAIN, 512):
        o_ref[:, col:col + 512] = _dot(xb, w_ref[:, col:col + 512]).astype(BF16)
    t_ref[...] = _dot(xb, wt_ref[...])


def _proj0(x2d, pos_rows, rope_c, w_main, w_tail, *, tm):
    T = x2d.shape[0]
    freq, expand, lane_c = rope_c
    return pl.pallas_call(
        _proj0_kernel,
        grid=(T // tm,),
        in_specs=[
            pl.BlockSpec((tm, D_MODEL), lambda i: (i, 0)),
            pl.BlockSpec((1, 1, tm), lambda i: (i, 0, 0)),
            _const_spec(freq.shape),
            _const_spec(expand.shape),
            _const_spec(lane_c.shape),
            _const_spec((D_MODEL, P0_MAIN)),
            _const_spec((D_MODEL, LANES)),
        ],
        out_specs=[
            pl.BlockSpec((tm, P0_MAIN), lambda i: (i, 0)),
            pl.BlockSpec((tm, LANES), lambda i: (i, 0)),
        ],
        out_shape=[
            jax.ShapeDtypeStruct((T, P0_MAIN), BF16),
            jax.ShapeDtypeStruct((T, LANES), F32),
        ],
        compiler_params=_cparams(("parallel",)),
        name="proj0",
    )(x2d, pos_rows, freq, expand, lane_c, w_main, w_tail)


QKV_C = 3 * H_C * DK_C
PROJ_CHUNK = 512


def _proj1_kernel(x_ref, w_ref, wt_ref, cw_ref, o_ref, t_ref, ybuf_sc, carry_sc, *, tm, tiles_per_seq):
    xb = x_ref[...]

    @pl.when((pl.program_id(0) % tiles_per_seq) == 0)
    def _():
        carry_sc[...] = jnp.zeros(carry_sc.shape, F32)

    def plain(col):
        o_ref[:, col:col + PROJ_CHUNK] = _dot(xb, w_ref[:, col:col + PROJ_CHUNK]).astype(BF16)

    def tail():
        t_ref[...] = _dot(xb, wt_ref[...])

    fillers = [functools.partial(plain, col) for col in range(QKV_C, P1_MAIN, PROJ_CHUNK)] + [tail]
    for ci, col in enumerate(range(0, QKV_C, PROJ_CHUNK)):
        ybuf_sc[0:8, :] = carry_sc[ci]
        ybuf_sc[8:8 + tm, :] = _dot(xb, w_ref[:, col:col + PROJ_CHUNK])
        carry_sc[ci] = ybuf_sc[tm:tm + 8, :]
        if ci % 2 == 0 and fillers:
            fillers.pop(0)()
        w = cw_ref[:, col:col + PROJ_CHUNK]
        y = w[CONV_C - 1:CONV_C, :] * ybuf_sc[8:8 + tm, :]
        for j in range(CONV_C - 1):
            y = y + w[j:j + 1, :] * ybuf_sc[5 + j:5 + j + tm, :]
        y = y * _sigmoid(y)
        if col < 2 * H_C * DK_C:
            scale = DK_C ** -0.5 if col < H_C * DK_C else 1.0
            for t in range(PROJ_CHUNK // LANES):
                yt = y[:, t * LANES:(t + 1) * LANES]
                yt = yt * (lax.rsqrt(jnp.sum(yt * yt, axis=-1, keepdims=True) + RMS_EPS) * scale)
                o_ref[:, col + t * LANES:col + (t + 1) * LANES] = yt.astype(BF16)
        else:
            o_ref[:, col:col + PROJ_CHUNK] = y.astype(BF16)
    for f in fillers:
        f()


def _proj1(xb2d, w_main, w_tail, conv_w, *, S, tm):
    T = xb2d.shape[0]
    kern = functools.partial(_proj1_kernel, tm=tm, tiles_per_seq=S // tm)
    return pl.pallas_call(
        kern,
        grid=(T // tm,),
        in_specs=[
            pl.BlockSpec((tm, D_MODEL), lambda i: (i, 0)),
            _const_spec((D_MODEL, P1_MAIN)),
            _const_spec((D_MODEL, LANES)),
            _const_spec((CONV_C, QKV_C)),
        ],
        out_specs=[
            pl.BlockSpec((tm, P1_MAIN), lambda i: (i, 0)),
            pl.BlockSpec((tm, LANES), lambda i: (i, 0)),
        ],
        out_shape=[
            jax.ShapeDtypeStruct((T, P1_MAIN), BF16),
            jax.ShapeDtypeStruct((T, LANES), F32),
        ],
        scratch_shapes=[
            pltpu.VMEM((tm + 8, PROJ_CHUNK), F32),
            pltpu.VMEM((QKV_C // PROJ_CHUNK, 8, PROJ_CHUNK), F32),
        ],
        compiler_params=_cparams(("arbitrary",)),
        name="proj1",
    )(xb2d, w_main, w_tail, conv_w)


def _kvproj_kernel(m_ref, w_ref, o_ref):
    mb = m_ref[...].astype(BF16)
    for col in range(0, 2 * D_MODEL, 512):
        o_ref[:, col:col + 512] = _dot(mb, w_ref[:, col:col + 512]).astype(BF16)


def _kvproj(mem2d, wkv, *, tm):
    R = mem2d.shape[0]
    return pl.pallas_call(
        _kvproj_kernel,
        grid=(R // tm,),
        in_specs=[
            pl.BlockSpec((tm, D_MODEL), lambda i: (i, 0)),
            _const_spec((D_MODEL, 2 * D_MODEL)),
        ],
        out_specs=pl.BlockSpec((tm, 2 * D_MODEL), lambda i: (i, 0)),
        out_shape=jax.ShapeDtypeStruct((R, 2 * D_MODEL), BF16),
        compiler_params=_cparams(("parallel",)),
        name="kvproj",
    )(mem2d, wkv)


def _diffattn_kernel(q_ref, k_ref, v_ref, lam_ref, g_ref, o_ref, *, S, tq, lam_init):
    lv = lam_ref[...]
    lam = (jnp.exp(jnp.sum(lv[0:1] * lv[1:2], axis=-1, keepdims=True))
           - jnp.exp(jnp.sum(lv[2:3] * lv[3:4], axis=-1, keepdims=True)) + lam_init)
    lane = lax.broadcasted_iota(jnp.int32, (tq, LANES), 1)
    r = lax.broadcasted_iota(jnp.int32, (2 * tq, tq), 0)
    c = lax.broadcasted_iota(jnp.int32, (2 * tq, tq), 1)
    causal = c <= jnp.where(r >= tq, r - tq, r)
    vt = v_ref[...].astype(F32).T
    vt_ext = jnp.concatenate([vt, jnp.ones((HALO, S), F32)], axis=0).astype(BF16)
    g_col = g_ref[...]
    def scores(qi):
        n_keys = (qi + 1) * tq
        q = q_ref[qi * tq:(qi + 1) * tq, :]
        zero = jnp.zeros_like(q)
        qq = jnp.concatenate([jnp.where(lane < D_A, q, zero), jnp.where(lane >= D_A, q, zero)], axis=0)
        s = _dot_nt(qq, k_ref[0:n_keys, :])
        s_diag = jnp.where(causal, s[:, n_keys - tq:], -jnp.inf)
        return s_diag if qi == 0 else jnp.concatenate([s[:, :n_keys - tq], s_diag], axis=1)

    def softmax_num(s):
        return jnp.exp(s - jnp.max(s, axis=-1, keepdims=True)).astype(BF16)

    def finish(qi, p):
        n_keys = (qi + 1) * tq
        ot = _dot_nt(vt_ext[:, 0:n_keys], p)
        o_all = ot[:DV_A] / ot[DV_A:DV_A + 1]
        o = o_all[:, :tq] - lam * o_all[:, tq:]
        ms = jnp.mean(o * o, axis=0, keepdims=True)
        y = o * lax.rsqrt(ms + RMS_EPS) * g_col * (1.0 - lam_init)
        o_ref[qi * tq:(qi + 1) * tq, :] = y.T.astype(BF16)

    nq = S // tq
    pairs = [(i, nq - 1 - i) for i in range(nq // 2)] + ([(nq // 2,)] if nq % 2 else [])
    order = [sum(pairs[i:i + 2], ()) for i in range(0, len(pairs), 2)]
    for group in order:
        ss = [scores(qi) for qi in group]
        ps = [softmax_num(s) for s in ss]
        for qi, p in zip(group, ps):
            finish(qi, p)


def _diffattn(hp, lam_vecs, norm_g, *, B, S, tq, lam_init):
    T = B * S
    kern = functools.partial(_diffattn_kernel, S=S, tq=tq, lam_init=lam_init)
    return pl.pallas_call(
        kern,
        grid=(B, H_A),
        in_specs=[
            pl.BlockSpec((S, LANES), lambda b, h: (b, h)),
            pl.BlockSpec((S, LANES), lambda b, h: (b, H_A + h)),
            pl.BlockSpec((S, LANES), lambda b, h: (b, 2 * H_A + h)),
            pl.BlockSpec((4, D_A), lambda b, h: (0, 0)),
            pl.BlockSpec((DV_A, 1), lambda b, h: (0, 0)),
        ],
        out_specs=pl.BlockSpec((S, LANES), lambda b, h: (b, h)),
        out_shape=jax.ShapeDtypeStruct((T, H_A * DV_A), BF16),
        compiler_params=_cparams(("parallel", "parallel")),
        name="diffattn",
    )(hp, hp, hp, lam_vecs, norm_g)


GLA_GROUP = 2 * CHUNK
GLA_LEVELS = (32, 16, 8, 4, 2, 1)


def _split_dot(m_bf16, hi, lo):
    return _dot(m_bf16, hi) + _dot(m_bf16, lo)


def _gla_kernel(q_ref, k_ref, v_ref, r_ref, t_ref, w2_ref, b2_ref, g_ref, o_ref, st_sc, *, rb):
    G = GLA_GROUP

    @pl.when(pl.program_id(1) == 0)
    def _():
        st_sc[...] = jnp.zeros(st_sc.shape, F32)

    r = lax.broadcasted_iota(jnp.int32, (G, G), 0)
    c = lax.broadcasted_iota(jnp.int32, (G, G), 1)
    tril2 = jnp.where(((r >= CHUNK) == (c >= CHUNK)) & (c <= r), 1.0, 0.0).astype(BF16)
    row = lax.broadcasted_iota(jnp.int32, (G, H_B * DK_B), 0)
    lane = lax.broadcasted_iota(jnp.int32, (G, LANES), 1)
    head_lanes = [lane < DK_B, lane >= DK_B]
    lane_c = lax.broadcasted_iota(jnp.int32, (CHUNK, LANES), 1)
    head_lanes_c = [lane_c < DK_B, lane_c >= DK_B]
    levels = []
    for w in GLA_LEVELS:
        same = (r // (2 * w)) == (c // (2 * w))
        r_in_i = (r % (2 * w)) >= w
        c_in_i = (c % (2 * w)) >= w
        in_range = (r_in_i & c_in_i & (c <= r)) | ((~r_in_i) & (~c_in_i) & (c > r))
        m_arg = jnp.where(same & in_range, 1.0, 0.0).astype(BF16)
        levels.append((m_arg, same, (row % (2 * w)) >= w))
    eye = r == c

    def group(gi, carry):
        r0 = pl.multiple_of(gi * G, G)
        rows = pl.ds(r0, G)
        x = _dot(t_ref[rows, :].astype(BF16), w2_ref[...]) + b2_ref[...]
        gk = (jnp.minimum(x, 0.0) - jnp.log(1.0 + jnp.exp(-jnp.abs(x)))) * (1.0 / GATE_NORMALIZER)
        hi = gk.astype(BF16)
        lo = (gk - hi.astype(F32)).astype(BF16)
        b = _split_dot(tril2, hi, lo)
        q = q_ref[rows, :].astype(F32) * (DK_B ** -0.5)
        k = k_ref[rows, :].astype(F32)
        v = v_ref[rows, :]
        q_lv, k_lv, masks = [], [], []
        for m_arg, same, row_in_i in levels:
            e = jnp.exp(_split_dot(m_arg, hi, lo))
            q_lv.append(jnp.where(row_in_i, q * e, 0.0))
            k_lv.append(jnp.where(row_in_i, 0.0, k * e).astype(BF16))
            masks.append(same)
        q_lv.append(q)
        k_lv.append(k.astype(BF16))
        masks.append(eye)
        atts = [None] * H_B
        for ql, kl_, mask in zip(q_lv, k_lv, masks):
            for h in range(H_B):
                blk = slice((h // 2) * LANES, (h // 2 + 1) * LANES)
                qm = jnp.where(head_lanes[h % 2], ql[:, blk], 0.0).astype(BF16)
                part = jnp.where(mask, _dot_nt(qm, kl_[:, blk]), 0.0)
                atts[h] = part if atts[h] is None else atts[h] + part
        avs = [_dot(atts[h].astype(BF16), v[:, h * DV_B:(h + 1) * DV_B]) for h in range(H_B)]
        qe = q * jnp.exp(b)
        sts = [st_sc[p] for p in range(H_B // 2)]
        inters = [[] for _ in range(H_B)]
        for c2 in range(G // CHUNK):
            cs = slice(c2 * CHUNK, (c2 + 1) * CHUNK)
            for h in range(H_B):
                blk = slice((h // 2) * LANES, (h // 2 + 1) * LANES)
                qm = jnp.where(head_lanes_c[h % 2], qe[cs, blk], 0.0).astype(BF16)
                inters[h].append(_dot_nt(qm, sts[h // 2].astype(BF16)))
            bl = b[(c2 + 1) * CHUNK - 1:(c2 + 1) * CHUNK, :]
            kl = (k[cs] * jnp.exp(bl - b[cs])).astype(BF16)
            ebl = jnp.exp(bl)
            kvs = [lax.dot_general(v[cs, h * DV_B:(h + 1) * DV_B], kl[:, (h // 2) * LANES:(h // 2 + 1) * LANES],
                                   TN_DIMS, preferred_element_type=F32) for h in range(H_B)]
            sts = [sts[p] * ebl[:, p * LANES:(p + 1) * LANES]
                   + jnp.where(head_lanes[0], kvs[2 * p], kvs[2 * p + 1]) for p in range(H_B // 2)]
        for p in range(H_B // 2):
            st_sc[p] = sts[p]
        for h in range(H_B):
            vs = slice(h * DV_B, (h + 1) * DV_B)
            o = jnp.concatenate(inters[h], axis=0) + avs[h]
            ms = jnp.mean(o * o, axis=-1, keepdims=True)
            gate = r_ref[rows, vs].astype(F32)
            y = o * lax.rsqrt(ms + RMS_EPS) * g_ref[...] * (gate * _sigmoid(gate))
            o_ref[rows, vs] = y.astype(BF16)
        return carry

    lax.fori_loop(0, rb // G, group, 0, unroll=4)


def _gla(hp, tail, w2p, b2, norm_g, *, B, S, rb):
    nb = S // rb
    T = B * S
    kern = functools.partial(_gla_kernel, rb=rb)
    return pl.pallas_call(
        kern,
        grid=(B, nb),
        in_specs=[
            pl.BlockSpec((rb, H_B * DK_B), lambda b, i: (b * nb + i, 6)),
            pl.BlockSpec((rb, H_B * DK_B), lambda b, i: (b * nb + i, 7)),
            pl.BlockSpec((rb, H_B * DV_B), lambda b, i: (b * nb + i, 4)),
            pl.BlockSpec((rb, H_B * DV_B), lambda b, i: (b * nb + i, 5)),
            pl.BlockSpec((rb, LANES), lambda b, i: (b * nb + i, 0)),
            pl.BlockSpec((LANES, H_B * DK_B), lambda b, i: (0, 0)),
            pl.BlockSpec((1, H_B * DK_B), lambda b, i: (0, 0)),
            pl.BlockSpec((1, DV_B), lambda b, i: (0, 0)),
        ],
        out_specs=pl.BlockSpec((rb, H_B * DV_B), lambda b, i: (b * nb + i, 0)),
        out_shape=jax.ShapeDtypeStruct((T, H_B * DV_B), BF16),
        scratch_shapes=[pltpu.VMEM((H_B // 2, DV_B, 2 * DK_B), F32)],
        compiler_params=_cparams(("parallel", "arbitrary")),
        name="gla",
    )(hp, hp, hp, hp, tail, w2p, b2, norm_g)


GDN_GROUP = 2 * CHUNK


def _gdn_kernel(q_ref, k_ref, v_ref, z_ref, t_ref, hp_ref, g_ref, o_ref, s_sc, *, rb):
    G = GDN_GROUP

    @pl.when(pl.program_id(1) == 0)
    def _():
        s_sc[...] = jnp.zeros(s_sc.shape, F32)

    r = lax.broadcasted_iota(jnp.int32, (G, G), 0)
    c = lax.broadcasted_iota(jnp.int32, (G, G), 1)
    same = (r >= CHUNK) == (c >= CHUNK)
    incl = same & (c <= r)
    strict = same & (c < r)
    tril2 = jnp.where(incl, 1.0, 0.0).astype(BF16)
    eye = (r == c).astype(F32)

    def group(gi, carry):
        r0 = pl.multiple_of(gi * G, G)
        rows = pl.ds(r0, G)
        t = t_ref[rows, :]
        beta_all = _sigmoid(t)
        g_all = -jnp.exp(hp_ref[0:1, :]) * _softplus(t + hp_ref[1:2, :])
        g_hi = g_all.astype(BF16)
        g_rest = g_all - g_hi.astype(F32)
        g_mid = g_rest.astype(BF16)
        g_lo = (g_rest - g_mid.astype(F32)).astype(BF16)
        cum_all = _dot(tril2, g_hi) + _dot(tril2, g_mid) + _dot(tril2, g_lo)
        cum_all_t = cum_all.T
        heads = range(H_C)
        khbs = [k_ref[rows, h * LANES:(h + 1) * LANES] for h in heads]
        qhbs = [q_ref[rows, h * LANES:(h + 1) * LANES] for h in heads]
        qs = [q.astype(F32) for q in qhbs]
        ks = [k.astype(F32) for k in khbs]
        vs = [v_ref[rows, h * LANES:(h + 1) * LANES].astype(F32) for h in heads]
        bhs = [jnp.broadcast_to(beta_all[:, h:h + 1], (G, LANES)) for h in heads]
        chs = [jnp.broadcast_to(cum_all[:, H_C + h:H_C + h + 1], (G, LANES)) for h in heads]
        crows = [jnp.broadcast_to(cum_all_t[H_C + h:H_C + h + 1, :], (G, G)) for h in heads]
        decs = [jnp.where(incl, jnp.exp(ch - crow), 0.0) for ch, crow in zip(chs, crows)]
        egs = [jnp.exp(ch) for ch in chs]
        kbs = [k * bh for k, bh in zip(ks, bhs)]
        kks = [_dot_nt(kb.astype(BF16), khb) for kb, khb in zip(kbs, khbs)]
        ps = [-jnp.where(strict, kk * dec, 0.0) for kk, dec in zip(kks, decs)]
        tinvs = [eye + p for p in ps]
        for _ in range(5):
            pbs = [p.astype(BF16) for p in ps]
            ps = [_dot(pb, pb) for pb in pbs]
            tinvs = [tinv + _dot(p.astype(BF16), tinv.astype(BF16)) for p, tinv in zip(ps, tinvs)]
        rhss = [jnp.concatenate([v * bh, kb * eg], axis=1).astype(BF16)
                for v, bh, kb, eg in zip(vs, bhs, kbs, egs)]
        uws = [_dot(tinv.astype(BF16), rhs) for tinv, rhs in zip(tinvs, rhss)]
        qks = [_dot_nt(qb, khb) * dec for qb, khb, dec in zip(qhbs, khbs, decs)]
        qgs = [q * eg for q, eg in zip(qs, egs)]
        sts = [s_sc[h] for h in heads]
        outs = [[] for _ in heads]
        for c2 in range(G // CHUNK):
            cs = slice(c2 * CHUNK, (c2 + 1) * CHUNK)
            wss = [_dot(jnp.concatenate([uw[cs, DV_C:], qg[cs]], axis=0).astype(BF16), st.astype(BF16))
                   for uw, qg, st in zip(uws, qgs, sts)]
            v_news = [uw[cs, :DV_C] - ws[:CHUNK] for uw, ws in zip(uws, wss)]
            for h in heads:
                outs[h].append(wss[h][CHUNK:] + _dot(qks[h][cs, cs].astype(BF16), v_news[h].astype(BF16)))
            gls = [ch[(c2 + 1) * CHUNK - 1:(c2 + 1) * CHUNK, :] for ch in chs]
            kgs = [k[cs] * jnp.exp(gl - ch[cs]) for k, gl, ch in zip(ks, gls, chs)]
            sts = [st * jnp.exp(gl) + _dot(kg.T.astype(BF16), v_new.astype(BF16))
                   for st, gl, kg, v_new in zip(sts, gls, kgs, v_news)]
        for h in heads:
            ls = slice(h * LANES, (h + 1) * LANES)
            s_sc[h] = sts[h]
            o = jnp.concatenate(outs[h], axis=0)
            ms = jnp.mean(o * o, axis=-1, keepdims=True)
            z = z_ref[pl.ds(r0, G), ls].astype(F32)
            y = o * lax.rsqrt(ms + RMS_EPS) * g_ref[...] * (z * _sigmoid(z))
            o_ref[pl.ds(r0, G), ls] = y.astype(BF16)
        return carry

    lax.fori_loop(0, rb // G, group, 0, unroll=2)


def _gdn(hp, tail, head_par, norm_g, *, B, S, rb):
    nb = S // rb
    T = B * S
    width = H_C * DK_C
    kern = functools.partial(_gdn_kernel, rb=rb)
    return pl.pallas_call(
        kern,
        grid=(B, nb),
        in_specs=[
            pl.BlockSpec((rb, width), lambda b, i: (b * nb + i, 0)),
            pl.BlockSpec((rb, width), lambda b, i: (b * nb + i, 1)),
            pl.BlockSpec((rb, width), lambda b, i: (b * nb + i, 2)),
            pl.BlockSpec((rb, width), lambda b, i: (b * nb + i, 3)),
            pl.BlockSpec((rb, LANES), lambda b, i: (b * nb + i, 0)),
            pl.BlockSpec((8, LANES), lambda b, i: (0, 0)),
            pl.BlockSpec((1, DV_C), lambda b, i: (0, 0)),
        ],
        out_specs=pl.BlockSpec((rb, width), lambda b, i: (b * nb + i, 0)),
        out_shape=jax.ShapeDtypeStruct((T, width), BF16),
        scratch_shapes=[pltpu.VMEM((H_C, DK_C, DV_C), F32)],
        compiler_params=_cparams(("parallel", "arbitrary")),
        name="gdn",
    )(hp, hp, hp, hp, tail, head_par, norm_g)


N_POST_PARAMS = 15


def _post_kernel(*refs, n_in, tm, tiles_per_seq, emit_bf16):
    ins = refs[:n_in]
    (wmix_ref, res_ref, g1_ref, b1_ref, kv_ref, wq_ref, wo_ref, g2_ref, b2_ref,
     win_ref, cw_ref, cb_ref, wout_ref, g3_ref, b3_ref) = refs[n_in:n_in + N_POST_PARAMS]
    n_out = 2 if emit_bf16 else 1
    outs = refs[n_in + N_POST_PARAMS:n_in + N_POST_PARAMS + n_out]
    att_sc, xcat_sc, hid_sc, act_sc = refs[n_in + N_POST_PARAMS + n_out:]

    y = ALPHA * res_ref[...]
    off = 0
    for r in ins:
        kdim = r.shape[1]
        y = y + _dot(r[...], wmix_ref[off:off + kdim, :])
        off += kdim
    h1 = _layer_norm(y, g1_ref[...], b1_ref[...])

    q = (_dot(h1.astype(BF16), wq_ref[...]) * (D_XA ** -0.5)).astype(BF16)
    for hd in range(N_XA):
        cs = slice(hd * D_XA, (hd + 1) * D_XA)
        s = _dot_nt(q[:, cs], kv_ref[:, cs])
        p = jnp.exp(s - jnp.max(s, axis=-1, keepdims=True))
        l = jnp.sum(p, axis=-1, keepdims=True)
        vs = slice(D_MODEL + hd * D_XA, D_MODEL + (hd + 1) * D_XA)
        att_sc[:, cs] = (_dot(p.astype(BF16), kv_ref[:, vs]) / l).astype(BF16)
    h2 = _layer_norm(ALPHA * h1 + _dot(att_sc[...], wo_ref[...]), g2_ref[...], b2_ref[...])

    is_start = (pl.program_id(0) % tiles_per_seq) == 0

    @pl.when(is_start)
    def _():
        xcat_sc[0:HALO, :] = jnp.zeros((HALO, D_MODEL), BF16)

    @pl.when(jnp.logical_not(is_start))
    def _():
        xcat_sc[0:HALO, :] = xcat_sc[tm:tm + HALO, :]

    xcat_sc[HALO:HALO + tm, :] = h2.astype(BF16)
    xc = xcat_sc[...]

    def conv(col):
        hid_sc[...] = _dot(xc, win_ref[:, col:col + FF_CHUNK])
        w = cw_ref[:, col:col + FF_CHUNK]
        y = cb_ref[:, col:col + FF_CHUNK] + w[FFN_CONV - 1:FFN_CONV, :] * hid_sc[HALO:HALO + tm, :]
        for j in range(FFN_CONV - 1):
            o = HALO - (FFN_CONV - 1) + j
            y = y + w[j:j + 1, :] * hid_sc[o:o + tm, :]
        return y

    for c in range(N_FF_CHUNKS):
        gate = conv(c * FF_CHUNK)
        up = conv(D_FF + c * FF_CHUNK)
        act_sc[:, c * FF_CHUNK:(c + 1) * FF_CHUNK] = (gate * _sigmoid(gate) * up).astype(BF16)
    out = _layer_norm(ALPHA * h2 + _dot(act_sc[...], wout_ref[...]), g3_ref[...], b3_ref[...])
    outs[0][...] = out
    if emit_bf16:
        outs[1][...] = out.astype(BF16)


def _post(ins, res, kv, params, *, S, tm, emit_bf16):
    (w_mix, g1, b1, wq, wo, g2, b2, w_in, cw, cb, w_out, g3, b3) = params
    T = res.shape[0]
    tiles_per_seq = S // tm
    kern = functools.partial(_post_kernel, n_in=len(ins), tm=tm, tiles_per_seq=tiles_per_seq, emit_bf16=emit_bf16)
    row_tile = pl.BlockSpec((tm, D_MODEL), lambda i: (i, 0))
    vec = pl.BlockSpec((1, D_MODEL), lambda i: (0, 0))
    in_specs = [pl.BlockSpec((tm, a.shape[1]), lambda i: (i, 0)) for a in ins]
    in_specs += [
        _const_spec(w_mix.shape), row_tile, vec, vec,
        pl.BlockSpec((N_MEM, 2 * D_MODEL), lambda i: (i // tiles_per_seq, 0)),
        _const_spec(wq.shape), _const_spec(wo.shape), vec, vec,
        _const_spec(w_in.shape), _const_spec(cw.shape), _const_spec(cb.shape), _const_spec(w_out.shape), vec, vec,
    ]
    out_specs = [row_tile]
    out_shape = [jax.ShapeDtypeStruct((T, D_MODEL), F32)]
    if emit_bf16:
        out_specs.append(row_tile)
        out_shape.append(jax.ShapeDtypeStruct((T, D_MODEL), BF16))
    return pl.pallas_call(
        kern,
        grid=(T // tm,),
        in_specs=in_specs,
        out_specs=out_specs,
        out_shape=out_shape,
        scratch_shapes=[
            pltpu.VMEM((tm, D_MODEL), BF16),
            pltpu.VMEM((tm + HALO, D_MODEL), BF16),
            pltpu.VMEM((tm + HALO, FF_CHUNK), F32),
            pltpu.VMEM((tm, D_FF), BF16),
        ],
        compiler_params=_cparams(("arbitrary",)),
        name="post",
    )(*ins, w_mix, res, g1, b1, kv, wq, wo, g2, b2, w_in, cw, cb, w_out, g3, b3)


def _rope_consts():
    half = ROPE_DIM // 2
    inv_freq = ROPE_THETA ** (-jnp.arange(0, ROPE_DIM, 2, dtype=F32) / ROPE_DIM)
    d = jnp.arange(LANES) % D_A
    sel = ((d[None, :] < ROPE_DIM) & (d[None, :] % half == jnp.arange(half)[:, None])).astype(F32)
    zero = jnp.zeros((half, LANES), F32)
    expand = jnp.concatenate([jnp.concatenate([sel, zero], axis=1),
                              jnp.concatenate([zero, sel], axis=1)], axis=0).astype(BF16)
    off = jnp.where(d >= ROPE_DIM, 1.0, 0.0)
    lo = jnp.where(d < half, -1.0, 0.0)
    hi = jnp.where((d >= half) & (d < ROPE_DIM), 1.0, 0.0)
    lane_c = jnp.zeros((8, LANES), F32).at[0].set(off).at[1].set(lo).at[2].set(hi)
    return inv_freq.reshape(half, 1), expand, lane_c


def _row(v):
    return v.reshape(1, -1).astype(F32)


def _tail_w(w, n_main):
    tail = w[:, n_main:]
    return jnp.pad(tail, ((0, 0), (0, LANES - tail.shape[1]))).astype(BF16)


def _post_layer(ins, res, mem2d, S, params, tm, emit_bf16):
    (w_mix, ln1_g, ln1_b, xa_wq, xa_wkv, xa_wo, ln2_g, ln2_b, f_in, f_cw, f_cb, f_out, ln3_g, ln3_b) = params
    kv = _kvproj(mem2d, xa_wkv.astype(BF16), tm=tm)
    prepared = (w_mix.astype(BF16), _row(ln1_g), _row(ln1_b), xa_wq.astype(BF16), xa_wo.astype(BF16),
                _row(ln2_g), _row(ln2_b), f_in.astype(BF16), f_cw.astype(F32), _row(f_cb), f_out.astype(BF16),
                _row(ln3_g), _row(ln3_b))
    return _post(ins, res, kv, prepared, S=S, tm=tm, emit_bf16=emit_bf16)


def kernel(x, mem, positions, w_in_0, diff_lambda_0, diff_norm_0, gla_w2_0, gla_b2_0, gla_norm_0, w_mix_out_0, ln1_g_0, ln1_b_0, xa_wq_0, xa_wkv_0, xa_wo_0, ln2_g_0, ln2_b_0, ffn_w_in_0, ffn_conv_w_0, ffn_conv_b_0, ffn_w_out_0, ln3_g_0, ln3_b_0, w_in_1, gdn_conv_w_1, gdn_a_log_1, gdn_dt_bias_1, gdn_norm_1, w_mix_out_1, ln1_g_1, ln1_b_1, xa_wq_1, xa_wkv_1, xa_wo_1, ln2_g_1, ln2_b_1, ffn_w_in_1, ffn_conv_w_1, ffn_conv_b_1, ffn_w_out_1, ln3_g_1, ln3_b_1):
    B, S, _ = x.shape
    T = B * S
    tm = min(512, S)
    x2d = x.reshape(T, D_MODEL)
    mem2d = mem.reshape(B * N_MEM, D_MODEL)

    hp, tail = _proj0(x2d, positions.reshape(T // tm, 1, tm), _rope_consts(),
                      w_in_0[:, :P0_MAIN].astype(BF16), _tail_w(w_in_0, P0_MAIN), tm=tm)
    lam_init = 0.8 - 0.6 * math.exp(-0.3 * 0)
    oa = _diffattn(hp, diff_lambda_0.astype(F32), diff_norm_0.reshape(DV_A, 1).astype(F32), B=B, S=S,
                   tq=min(128, S), lam_init=lam_init)
    w2p = jnp.pad(gla_w2_0, ((0, LANES - GATE_RANK), (0, 0))).astype(BF16)
    ob = _gla(hp, tail, w2p, _row(gla_b2_0), _row(gla_norm_0), B=B, S=S, rb=min(512, S))
    h, hb = _post_layer([oa, ob], x2d, mem2d, S,
                        (w_mix_out_0, ln1_g_0, ln1_b_0, xa_wq_0, xa_wkv_0, xa_wo_0, ln2_g_0, ln2_b_0,
                         ffn_w_in_0, ffn_conv_w_0, ffn_conv_b_0, ffn_w_out_0, ln3_g_0, ln3_b_0), tm, True)

    hp, tail = _proj1(hb, w_in_1[:, :P1_MAIN].astype(BF16), _tail_w(w_in_1, P1_MAIN), gdn_conv_w_1.astype(F32),
                      S=S, tm=tm)
    head_par = (jnp.zeros((8, LANES), F32)
                .at[0, H_C:2 * H_C].set(gdn_a_log_1.astype(F32))
                .at[1, H_C:2 * H_C].set(gdn_dt_bias_1.astype(F32)))
    oc = _gdn(hp, tail, head_par, _row(gdn_norm_1), B=B, S=S, rb=min(512, S))
    (out,) = _post_layer([oc], h, mem2d, S,
                         (w_mix_out_1, ln1_g_1, ln1_b_1, xa_wq_1, xa_wkv_1, xa_wo_1, ln2_g_1, ln2_b_1,
                          ffn_w_in_1, ffn_conv_w_1, ffn_conv_b_1, ffn_w_out_1, ln3_g_1, ln3_b_1), tm, False)
    return out.reshape(B, S, D_MODEL)
```

```python
import functools
import math

import jax
import jax.numpy as jnp
from jax import lax
from jax.experimental import pallas as pl
from jax.experimental.pallas import tpu as pltpu

F32 = jnp.float32
BF16 = jnp.bfloat16

D_MODEL = 1024
N_MEM = 256
H_A, D_A, DV_A = 4, 64, 128
ROPE_THETA, ROPE_DIM = 500000.0, 16
H_B, DK_B, DV_B = 4, 64, 128
GATE_RANK, GATE_NORMALIZER = 16, 16.0
H_C, DK_C, DV_C, CONV_C = 8, 128, 128, 4
CHUNK = 64
N_XA, D_XA = 4, 256
D_FF = 2816
FFN_CONV = 3
LN_EPS, RMS_EPS = 1e-5, 1e-6
DEPTH = 2
ALPHA = (2.0 * DEPTH) ** 0.25
P0_MAIN = 3072
P1_MAIN = 4096
LANES = 128
FF_CHUNK = 256
N_FF_CHUNKS = D_FF // FF_CHUNK
HALO = 16
VMEM_LIMIT = 56 * 1024 * 1024

NT_DIMS = (((1,), (1,)), ((), ()))
TN_DIMS = (((0,), (0,)), ((), ()))
HI = lax.Precision.HIGHEST


def _cparams(sem):
    return pltpu.CompilerParams(dimension_semantics=sem, vmem_limit_bytes=VMEM_LIMIT)


def _const_spec(shape):
    nd = len(shape)
    return pl.BlockSpec(shape, lambda *_: (0,) * nd, pipeline_mode=pl.Buffered(1))


def _dot(a, b):
    return jnp.dot(a, b, preferred_element_type=F32)


def _dot_nt(a, b, precision=None):
    return lax.dot_general(a, b, NT_DIMS, precision=precision, preferred_element_type=F32)


def _layer_norm(y, g, b):
    mu = jnp.mean(y, axis=-1, keepdims=True)
    yc = y - mu
    var = jnp.mean(yc * yc, axis=-1, keepdims=True)
    return yc * lax.rsqrt(var + LN_EPS) * g + b


def _sigmoid(x):
    return 1.0 / (1.0 + jnp.exp(-x))


def _softplus(x):
    return jnp.maximum(x, 0.0) + jnp.log(1.0 + jnp.exp(-jnp.abs(x)))


def _tril_mask(n, strict=False):
    r = lax.broadcasted_iota(jnp.int32, (n, n), 0)
    c = lax.broadcasted_iota(jnp.int32, (n, n), 1)
    return (c < r) if strict else (c <= r)


def _proj0_kernel(x_ref, pos_ref, freq_ref, expand_ref, rc_ref, w_ref, wt_ref, o_ref, t_ref):
    xb = x_ref[...].astype(BF16)
    ang_t = freq_ref[...] * pos_ref[0].astype(F32)
    trig_t = jnp.concatenate([jnp.cos(ang_t), jnp.sin(ang_t)], axis=0)
    hi = trig_t.astype(BF16)
    lo = (trig_t - hi.astype(F32)).astype(BF16)
    e = expand_ref[...]
    pat = (lax.dot_general(hi, e, TN_DIMS, preferred_element_type=F32)
           + lax.dot_general(lo, e, TN_DIMS, preferred_element_type=F32))
    c = pat[:, 0:LANES] + rc_ref[0:1, :]
    s_lo = pat[:, LANES:2 * LANES] * rc_ref[1:2, :]
    s_hi = pat[:, LANES:2 * LANES] * rc_ref[2:3, :]
    q_scale = D_A ** -0.5
    for j in range(2):
        y = _dot(xb, w_ref[:, j * 512:(j + 1) * 512])
        for t in range(4):
            yt = y[:, t * LANES:(t + 1) * LANES]
            yt = yt * c + pltpu.roll(yt, LANES - 8, 1) * s_lo + pltpu.roll(yt, 8, 1) * s_hi
            if j == 0:
                yt = yt * q_scale
            col = j * 512 + t * LANES
            o_ref[:, col:col + LANES] = yt.astype(BF16)
    for col in range(1024, P0_MAIN, 512):
        o_ref[:, col:col + 512] = _dot(xb, w_ref[:, col:col + 512]).astype(BF16)
    t_ref[...] = _dot(xb, wt_ref[...])


def _proj0(x2d, pos_rows, rope_c, w_main, w_tail, *, tm):
    T = x2d.shape[0]
    freq, expand, lane_c = rope_c
    return pl.pallas_call(
        _proj0_kernel,
        grid=(T // tm,),
        in_specs=[
            pl.BlockSpec((tm, D_MODEL), lambda i: (i, 0)),
            pl.BlockSpec((1, 1, tm), lambda i: (i, 0, 0)),
            _const_spec(freq.shape),
            _const_spec(expand.shape),
            _const_spec(lane_c.shape),
            _const_spec((D_MODEL, P0_MAIN)),
            _const_spec((D_MODEL, LANES)),
        ],
        out_specs=[
            pl.BlockSpec((tm, P0_MAIN), lambda i: (i, 0)),
            pl.BlockSpec((tm, LANES), lambda i: (i, 0)),
        ],
        out_shape=[
            jax.ShapeDtypeStruct((T, P0_MAIN), BF16),
            jax.ShapeDtypeStruct((T, LANES), F32),
        ],
        compiler_params=_cparams(("parallel",)),
        name="proj0",
    )(x2d, pos_rows, freq, expand, lane_c, w_main, w_tail)


QKV_C = 3 * H_C * DK_C
PROJ_CHUNK = 512


def _proj1_kernel(x_ref, w_ref, wt_ref, cw_ref, o_ref, t_ref, ybuf_sc, carry_sc, *, tm, tiles_per_seq):
    xb = x_ref[...]

    @pl.when((pl.program_id(0) % tiles_per_seq) == 0)
    def _():
        carry_sc[...] = jnp.zeros(carry_sc.shape, F32)

    def plain(col):
        o_ref[:, col:col + PROJ_CHUNK] = _dot(xb, w_ref[:, col:col + PROJ_CHUNK]).astype(BF16)

    def tail():
        t_ref[...] = _dot(xb, wt_ref[...])

    fillers = [functools.partial(plain, col) for col in range(QKV_C, P1_MAIN, PROJ_CHUNK)] + [tail]
    for ci, col in enumerate(range(0, QKV_C, PROJ_CHUNK)):
        ybuf_sc[0:8, :] = carry_sc[ci]
        ybuf_sc[8:8 + tm, :] = _dot(xb, w_ref[:, col:col + PROJ_CHUNK])
        carry_sc[ci] = ybuf_sc[tm:tm + 8, :]
        if ci % 2 == 0 and fillers:
            fillers.pop(0)()
        w = cw_ref[:, col:col + PROJ_CHUNK]
        y = w[CONV_C - 1:CONV_C, :] * ybuf_sc[8:8 + tm, :]
        for j in range(CONV_C - 1):
            y = y + w[j:j + 1, :] * ybuf_sc[5 + j:5 + j + tm, :]
        y = y * _sigmoid(y)
        if col < 2 * H_C * DK_C:
            scale = DK_C ** -0.5 if col < H_C * DK_C else 1.0
            for t in range(PROJ_CHUNK // LANES):
                yt = y[:, t * LANES:(t + 1) * LANES]
                yt = yt * (lax.rsqrt(jnp.sum(yt * yt, axis=-1, keepdims=True) + RMS_EPS) * scale)
                o_ref[:, col + t * LANES:col + (t + 1) * LANES] = yt.astype(BF16)
        else:
            o_ref[:, col:col + PROJ_CHUNK] = y.astype(BF16)
    for f in fillers:
        f()


def _proj1(xb2d, w_main, w_tail, conv_w, *, S, tm):
    T = xb2d.shape[0]
    kern = functools.partial(_proj1_kernel, tm=tm, tiles_per_seq=S // tm)
    return pl.pallas_call(
        kern,
        grid=(T // tm,),
        in_specs=[
            pl.BlockSpec((tm, D_MODEL), lambda i: (i, 0)),
            _const_spec((D_MODEL, P1_MAIN)),
            _const_spec((D_MODEL, LANES)),
            _const_spec((CONV_C, QKV_C)),
        ],
        out_specs=[
            pl.BlockSpec((tm, P1_MAIN), lambda i: (i, 0)),
            pl.BlockSpec((tm, LANES), lambda i: (i, 0)),
        ],
        out_shape=[
            jax.ShapeDtypeStruct((T, P1_MAIN), BF16),
            jax.ShapeDtypeStruct((T, LANES), F32),
        ],
        scratch_shapes=[
            pltpu.VMEM((tm + 8, PROJ_CHUNK), F32),
            pltpu.VMEM((QKV_C // PROJ_CHUNK, 8, PROJ_CHUNK), F32),
        ],
        compiler_params=_cparams(("arbitrary",)),
        name="proj1",
    )(xb2d, w_main, w_tail, conv_w)


def _kvproj_kernel(m_ref, w_ref, o_ref):
    mb = m_ref[...].astype(BF16)
    for col in range(0, 2 * D_MODEL, 512):
        o_ref[:, col:col + 512] = _dot(mb, w_ref[:, col:col + 512]).astype(BF16)


def _kvproj(mem2d, wkv, *, tm):
    R = mem2d.shape[0]
    return pl.pallas_call(
        _kvproj_kernel,
        grid=(R // tm,),
        in_specs=[
            pl.BlockSpec((tm, D_MODEL), lambda i: (i, 0)),
            _const_spec((D_MODEL, 2 * D_MODEL)),
        ],
        out_specs=pl.BlockSpec((tm, 2 * D_MODEL), lambda i: (i, 0)),
        out_shape=jax.ShapeDtypeStruct((R, 2 * D_MODEL), BF16),
        compiler_params=_cparams(("parallel",)),
        name="kvproj",
    )(mem2d, wkv)


def _diffattn_kernel(q_ref, k_ref, v_ref, lam_ref, g_ref, o_ref, *, S, tq, lam_init):
    lv = lam_ref[...]
    lam = (jnp.exp(jnp.sum(lv[0:1] * lv[1:2], axis=-1, keepdims=True))
           - jnp.exp(jnp.sum(lv[2:3] * lv[3:4], axis=-1, keepdims=True)) + lam_init)
    lane = lax.broadcasted_iota(jnp.int32, (tq, LANES), 1)
    r = lax.broadcasted_iota(jnp.int32, (2 * tq, tq), 0)
    c = lax.broadcasted_iota(jnp.int32, (2 * tq, tq), 1)
    causal = c <= jnp.where(r >= tq, r - tq, r)
    vt = v_ref[...].astype(F32).T
    vt_ext = jnp.concatenate([vt, jnp.ones((HALO, S), F32)], axis=0).astype(BF16)
    g_col = g_ref[...]
    def scores(qi):
        n_keys = (qi + 1) * tq
        q = q_ref[qi * tq:(qi + 1) * tq, :]
        zero = jnp.zeros_like(q)
        qq = jnp.concatenate([jnp.where(lane < D_A, q, zero), jnp.where(lane >= D_A, q, zero)], axis=0)
        s = _dot_nt(qq, k_ref[0:n_keys, :])
        s_diag = jnp.where(causal, s[:, n_keys - tq:], -jnp.inf)
        return s_diag if qi == 0 else jnp.concatenate([s[:, :n_keys - tq], s_diag], axis=1)

    def softmax_num(s):
        return jnp.exp(s - jnp.max(s, axis=-1, keepdims=True)).astype(BF16)

    def finish(qi, p):
        n_keys = (qi + 1) * tq
        ot = _dot_nt(vt_ext[:, 0:n_keys], p)
        o_all = ot[:DV_A] / ot[DV_A:DV_A + 1]
        o = o_all[:, :tq] - lam * o_all[:, tq:]
        ms = jnp.mean(o * o, axis=0, keepdims=True)
        y = o * lax.rsqrt(ms + RMS_EPS) * g_col * (1.0 - lam_init)
        o_ref[qi * tq:(qi + 1) * tq, :] = y.T.astype(BF16)

    nq = S // tq
    pairs = [(i, nq - 1 - i) for i in range(nq // 2)] + ([(nq // 2,)] if nq % 2 else [])
    order = [sum(pairs[i:i + 2], ()) for i in range(0, len(pairs), 2)]
    for group in order:
        ss = [scores(qi) for qi in group]
        ps = [softmax_num(s) for s in ss]
        for qi, p in zip(group, ps):
            finish(qi, p)


def _diffattn(hp, lam_vecs, norm_g, *, B, S, tq, lam_init):
    T = B * S
    kern = functools.partial(_diffattn_kernel, S=S, tq=tq, lam_init=lam_init)
    return pl.pallas_call(
        kern,
        grid=(B, H_A),
        in_specs=[
            pl.BlockSpec((S, LANES), lambda b, h: (b, h)),
            pl.BlockSpec((S, LANES), lambda b, h: (b, H_A + h)),
            pl.BlockSpec((S, LANES), lambda b, h: (b, 2 * H_A + h)),
            pl.BlockSpec((4, D_A), lambda b, h: (0, 0)),
            pl.BlockSpec((DV_A, 1), lambda b, h: (0, 0)),
        ],
        out_specs=pl.BlockSpec((S, LANES), lambda b, h: (b, h)),
        out_shape=jax.ShapeDtypeStruct((T, H_A * DV_A), BF16),
        compiler_params=_cparams(("parallel", "parallel")),
        name="diffattn",
    )(hp, hp, hp, lam_vecs, norm_g)


GLA_GROUP = 2 * CHUNK
GLA_LEVELS = (32, 16, 8, 4, 2, 1)


def _split_dot(m_bf16, hi, lo):
    return _dot(m_bf16, hi) + _dot(m_bf16, lo)


def _gla_kernel(q_ref, k_ref, v_ref, r_ref, t_ref, w2_ref, b2_ref, g_ref, o_ref, st_sc, *, rb):
    G = GLA_GROUP

    @pl.when(pl.program_id(1) == 0)
    def _():
        st_sc[...] = jnp.zeros(st_sc.shape, F32)

    r = lax.broadcasted_iota(jnp.int32, (G, G), 0)
    c = lax.broadcasted_iota(jnp.int32, (G, G), 1)
    tril2 = jnp.where(((r >= CHUNK) == (c >= CHUNK)) & (c <= r), 1.0, 0.0).astype(BF16)
    row = lax.broadcasted_iota(jnp.int32, (G, H_B * DK_B), 0)
    lane = lax.broadcasted_iota(jnp.int32, (G, LANES), 1)
    head_lanes = [lane < DK_B, lane >= DK_B]
    lane_c = lax.broadcasted_iota(jnp.int32, (CHUNK, LANES), 1)
    head_lanes_c = [lane_c < DK_B, lane_c >= DK_B]
    levels = []
    for w in GLA_LEVELS:
        same = (r // (2 * w)) == (c // (2 * w))
        r_in_i = (r % (2 * w)) >= w
        c_in_i = (c % (2 * w)) >= w
        in_range = (r_in_i & c_in_i & (c <= r)) | ((~r_in_i) & (~c_in_i) & (c > r))
        m_arg = jnp.where(same & in_range, 1.0, 0.0).astype(BF16)
        levels.append((m_arg, same, (row % (2 * w)) >= w))
    eye = r == c

    def group(gi, carry):
        r0 = pl.multiple_of(gi * G, G)
        rows = pl.ds(r0, G)
        x = _dot(t_ref[rows, :].astype(BF16), w2_ref[...]) + b2_ref[...]
        gk = (jnp.minimum(x, 0.0) - jnp.log(1.0 + jnp.exp(-jnp.abs(x)))) * (1.0 / GATE_NORMALIZER)
        hi = gk.astype(BF16)
        lo = (gk - hi.astype(F32)).astype(BF16)
        b = _split_dot(tril2, hi, lo)
        q = q_ref[rows, :].astype(F32) * (DK_B ** -0.5)
        k = k_ref[rows, :].astype(F32)
        v = v_ref[rows, :]
        q_lv, k_lv, masks = [], [], []
        for m_arg, same, row_in_i in levels:
            e = jnp.exp(_split_dot(m_arg, hi, lo))
            q_lv.append(jnp.where(row_in_i, q * e, 0.0))
            k_lv.append(jnp.where(row_in_i, 0.0, k * e).astype(BF16))
            masks.append(same)
        q_lv.append(q)
        k_lv.append(k.astype(BF16))
        masks.append(eye)
        atts = [None] * H_B
        for ql, kl_, mask in zip(q_lv, k_lv, masks):
            for h in range(H_B):
                blk = slice((h // 2) * LANES, (h // 2 + 1) * LANES)
                qm = jnp.where(head_lanes[h % 2], ql[:, blk], 0.0).astype(BF16)
                part = jnp.where(mask, _dot_nt(qm, kl_[:, blk]), 0.0)
                atts[h] = part if atts[h] is None else atts[h] + part
        avs = [_dot(atts[h].astype(BF16), v[:, h * DV_B:(h + 1) * DV_B]) for h in range(H_B)]
        qe = q * jnp.exp(b)
        sts = [st_sc[p] for p in range(H_B // 2)]
        inters = [[] for _ in range(H_B)]
        for c2 in range(G // CHUNK):
            cs = slice(c2 * CHUNK, (c2 + 1) * CHUNK)
            for h in range(H_B):
                blk = slice((h // 2) * LANES, (h // 2 + 1) * LANES)
                qm = jnp.where(head_lanes_c[h % 2], qe[cs, blk], 0.0).astype(BF16)
                inters[h].append(_dot_nt(qm, sts[h // 2].astype(BF16)))
            bl = b[(c2 + 1) * CHUNK - 1:(c2 + 1) * CHUNK, :]
            kl = (k[cs] * jnp.exp(bl - b[cs])).astype(BF16)
            ebl = jnp.exp(bl)
            kvs = [lax.dot_general(v[cs, h * DV_B:(h + 1) * DV_B], kl[:, (h // 2) * LANES:(h // 2 + 1) * LANES],
                                   TN_DIMS, preferred_element_type=F32) for h in range(H_B)]
            sts = [sts[p] * ebl[:, p * LANES:(p + 1) * LANES]
                   + jnp.where(head_lanes[0], kvs[2 * p], kvs[2 * p + 1]) for p in range(H_B // 2)]
        for p in range(H_B // 2):
            st_sc[p] = sts[p]
        for h in range(H_B):
            vs = slice(h * DV_B, (h + 1) * DV_B)
            o = jnp.concatenate(inters[h], axis=0) + avs[h]
            ms = jnp.mean(o * o, axis=-1, keepdims=True)
            gate = r_ref[rows, vs].astype(F32)
            y = o * lax.rsqrt(ms + RMS_EPS) * g_ref[...] * (gate * _sigmoid(gate))
            o_ref[rows, vs] = y.astype(BF16)
        return carry

    lax.fori_loop(0, rb // G, group, 0, unroll=4)


def _gla(hp, tail, w2p, b2, norm_g, *, B, S, rb):
    nb = S // rb
    T = B * S
    kern = functools.partial(_gla_kernel, rb=rb)
    return pl.pallas_call(
        kern,
        grid=(B, nb),
        in_specs=[
            pl.BlockSpec((rb, H_B * DK_B), lambda b, i: (b * nb + i, 6)),
            pl.BlockSpec((rb, H_B * DK_B), lambda b, i: (b * nb + i, 7)),
            pl.BlockSpec((rb, H_B * DV_B), lambda b, i: (b * nb + i, 4)),
            pl.BlockSpec((rb, H_B * DV_B), lambda b, i: (b * nb + i, 5)),
            pl.BlockSpec((rb, LANES), lambda b, i: (b * nb + i, 0)),
            pl.BlockSpec((LANES, H_B * DK_B), lambda b, i: (0, 0)),
            pl.BlockSpec((1, H_B * DK_B), lambda b, i: (0, 0)),
            pl.BlockSpec((1, DV_B), lambda b, i: (0, 0)),
        ],
        out_specs=pl.BlockSpec((rb, H_B * DV_B), lambda b, i: (b * nb + i, 0)),
        out_shape=jax.ShapeDtypeStruct((T, H_B * DV_B), BF16),
        scratch_shapes=[pltpu.VMEM((H_B // 2, DV_B, 2 * DK_B), F32)],
        compiler_params=_cparams(("parallel", "arbitrary")),
        name="gla",
    )(hp, hp, hp, hp, tail, w2p, b2, norm_g)


GDN_GROUP = 2 * CHUNK


def _gdn_kernel(q_ref, k_ref, v_ref, z_ref, t_ref, hp_ref, g_ref, o_ref, s_sc, *, rb):
    G = GDN_GROUP

    @pl.when(pl.program_id(1) == 0)
    def _():
        s_sc[...] = jnp.zeros(s_sc.shape, F32)

    r = lax.broadcasted_iota(jnp.int32, (G, G), 0)
    c = lax.broadcasted_iota(jnp.int32, (G, G), 1)
    same = (r >= CHUNK) == (c >= CHUNK)
    incl = same & (c <= r)
    strict = same & (c < r)
    tril2 = jnp.where(incl, 1.0, 0.0).astype(BF16)
    eye = (r == c).astype(F32)

    def group(gi, carry):
        r0 = pl.multiple_of(gi * G, G)
        rows = pl.ds(r0, G)
        t = t_ref[rows, :]
        beta_all = _sigmoid(t)
        g_all = -jnp.exp(hp_ref[0:1, :]) * _softplus(t + hp_ref[1:2, :])
        g_hi = g_all.astype(BF16)
        g_rest = g_all - g_hi.astype(F32)
        g_mid = g_rest.astype(BF16)
        g_lo = (g_rest - g_mid.astype(F32)).astype(BF16)
        cum_all = _dot(tril2, g_hi) + _dot(tril2, g_mid) + _dot(tril2, g_lo)
        cum_all_t = cum_all.T
        heads = range(H_C)
        khbs = [k_ref[rows, h * LANES:(h + 1) * LANES] for h in heads]
        qhbs = [q_ref[rows, h * LANES:(h + 1) * LANES] for h in heads]
        qs = [q.astype(F32) for q in qhbs]
        ks = [k.astype(F32) for k in khbs]
        vs = [v_ref[rows, h * LANES:(h + 1) * LANES].astype(F32) for h in heads]
        bhs = [jnp.broadcast_to(beta_all[:, h:h + 1], (G, LANES)) for h in heads]
        chs = [jnp.broadcast_to(cum_all[:, H_C + h:H_C + h + 1], (G, LANES)) for h in heads]
        crows = [jnp.broadcast_to(cum_all_t[H_C + h:H_C + h + 1, :], (G, G)) for h in heads]
        decs = [jnp.where(incl, jnp.exp(ch - crow), 0.0) for ch, crow in zip(chs, crows)]
        egs = [jnp.exp(ch) for ch in chs]
        kbs = [k * bh for k, bh in zip(ks, bhs)]
        kks = [_dot_nt(kb.astype(BF16), khb) for kb, khb in zip(kbs, khbs)]
        ps = [-jnp.where(strict, kk * dec, 0.0) for kk, dec in zip(kks, decs)]
        tinvs = [eye + p for p in ps]
        for _ in range(5):
            pbs = [p.astype(BF16) for p in ps]
            ps = [_dot(pb, pb) for pb in pbs]
            tinvs = [tinv + _dot(p.astype(BF16), tinv.astype(BF16)) for p, tinv in zip(ps, tinvs)]
        rhss = [jnp.concatenate([v * bh, kb * eg], axis=1).astype(BF16)
                for v, bh, kb, eg in zip(vs, bhs, kbs, egs)]
        uws = [_dot(tinv.astype(BF16), rhs) for tinv, rhs in zip(tinvs, rhss)]
        qks = [_dot_nt(qb, khb) * dec for qb, khb, dec in zip(qhbs, khbs, decs)]
        qgs = [q * eg for q, eg in zip(qs, egs)]
        sts = [s_sc[h] for h in heads]
        outs = [[] for _ in heads]
        for c2 in range(G // CHUNK):
            cs = slice(c2 * CHUNK, (c2 + 1) * CHUNK)
            wss = [_dot(jnp.concatenate([uw[cs, DV_C:], qg[cs]], axis=0).astype(BF16), st.astype(BF16))
                   for uw, qg, st in zip(uws, qgs, sts)]
            v_news = [uw[cs, :DV_C] - ws[:CHUNK] for uw, ws in zip(uws, wss)]
            for h in heads:
                outs[h].append(wss[h][CHUNK:] + _dot(qks[h][cs, cs].astype(BF16), v_news[h].astype(BF16)))
            gls = [ch[(c2 + 1) * CHUNK - 1:(c2 + 1) * CHUNK, :] for ch in chs]
            kgs = [k[cs] * jnp.exp(gl - ch[cs]) for k, gl, ch in zip(ks, gls, chs)]
            sts = [st * jnp.exp(gl) + _dot(kg.T.astype(BF16), v_new.astype(BF16))
                   for st, gl, kg, v_new in zip(sts, gls, kgs, v_news)]
        for h in heads:
            ls = slice(h * LANES, (h + 1) * LANES)
            s_sc[h] = sts[h]
            o = jnp.concatenate(outs[h], axis=0)
            ms = jnp.mean(o * o, axis=-1, keepdims=True)
            z = z_ref[pl.ds(r0, G), ls].astype(F32)
            y = o * lax.rsqrt(ms + RMS_EPS) * g_ref[...] * (z * _sigmoid(z))
            o_ref[pl.ds(r0, G), ls] = y.astype(BF16)
        return carry

    lax.fori_loop(0, rb // G, group, 0, unroll=2)


def _gdn(hp, tail, head_par, norm_g, *, B, S, rb):
    nb = S // rb
    T = B * S
    width = H_C * DK_C
    kern = functools.partial(_gdn_kernel, rb=rb)
    return pl.pallas_call(
        kern,
        grid=(B, nb),
        in_specs=[
            pl.BlockSpec((rb, width), lambda b, i: (b * nb + i, 0)),
            pl.BlockSpec((rb, width), lambda b, i: (b * nb + i, 1)),
            pl.BlockSpec((rb, width), lambda b, i: (b * nb + i, 2)),
            pl.BlockSpec((rb, width), lambda b, i: (b * nb + i, 3)),
            pl.BlockSpec((rb, LANES), lambda b, i: (b * nb + i, 0)),
            pl.BlockSpec((8, LANES), lambda b, i: (0, 0)),
            pl.BlockSpec((1, DV_C), lambda b, i: (0, 0)),
        ],
        out_specs=pl.BlockSpec((rb, width), lambda b, i: (b * nb + i, 0)),
        out_shape=jax.ShapeDtypeStruct((T, width), BF16),
        scratch_shapes=[pltpu.VMEM((H_C, DK_C, DV_C), F32)],
        compiler_params=_cparams(("parallel", "arbitrary")),
        name="gdn",
    )(hp, hp, hp, hp, tail, head_par, norm_g)


N_POST_PARAMS = 15


def _post_kernel(*refs, n_in, tm, tiles_per_seq, emit_bf16):
    ins = refs[:n_in]
    (wmix_ref, res_ref, g1_ref, b1_ref, kv_ref, wq_ref, wo_ref, g2_ref, b2_ref,
     win_ref, cw_ref, cb_ref, wout_ref, g3_ref, b3_ref) = refs[n_in:n_in + N_POST_PARAMS]
    n_out = 2 if emit_bf16 else 1
    outs = refs[n_in + N_POST_PARAMS:n_in + N_POST_PARAMS + n_out]
    att_sc, xcat_sc, hid_sc, act_sc = refs[n_in + N_POST_PARAMS + n_out:]

    y = ALPHA * res_ref[...]
    off = 0
    for r in ins:
        kdim = r.shape[1]
        y = y + _dot(r[...], wmix_ref[off:off + kdim, :])
        off += kdim
    h1 = _layer_norm(y, g1_ref[...], b1_ref[...])

    q = (_dot(h1.astype(BF16), wq_ref[...]) * (D_XA ** -0.5)).astype(BF16)
    for hd in range(N_XA):
        cs = slice(hd * D_XA, (hd + 1) * D_XA)
        s = _dot_nt(q[:, cs], kv_ref[:, cs])
        p = jnp.exp(s - jnp.max(s, axis=-1, keepdims=True))
        l = jnp.sum(p, axis=-1, keepdims=True)
        vs = slice(D_MODEL + hd * D_XA, D_MODEL + (hd + 1) * D_XA)
        att_sc[:, cs] = (_dot(p.astype(BF16), kv_ref[:, vs]) / l).astype(BF16)
    h2 = _layer_norm(ALPHA * h1 + _dot(att_sc[...], wo_ref[...]), g2_ref[...], b2_ref[...])

    is_start = (pl.program_id(0) % tiles_per_seq) == 0

    @pl.when(is_start)
    def _():
        xcat_sc[0:HALO, :] = jnp.zeros((HALO, D_MODEL), BF16)

    @pl.when(jnp.logical_not(is_start))
    def _():
        xcat_sc[0:HALO, :] = xcat_sc[tm:tm + HALO, :]

    xcat_sc[HALO:HALO + tm, :] = h2.astype(BF16)
    xc = xcat_sc[...]

    def conv(col):
        hid_sc[...] = _dot(xc, win_ref[:, col:col + FF_CHUNK])
        w = cw_ref[:, col:col + FF_CHUNK]
        y = cb_ref[:, col:col + FF_CHUNK] + w[FFN_CONV - 1:FFN_CONV, :] * hid_sc[HALO:HALO + tm, :]
        for j in range(FFN_CONV - 1):
            o = HALO - (FFN_CONV - 1) + j
            y = y + w[j:j + 1, :] * hid_sc[o:o + tm, :]
        return y

    for c in range(N_FF_CHUNKS):
        gate = conv(c * FF_CHUNK)
        up = conv(D_FF + c * FF_CHUNK)
        act_sc[:, c * FF_CHUNK:(c + 1) * FF_CHUNK] = (gate * _sigmoid(gate) * up).astype(BF16)
    out = _layer_norm(ALPHA * h2 + _dot(act_sc[...], wout_ref[...]), g3_ref[...], b3_ref[...])
    outs[0][...] = out
    if emit_bf16:
        outs[1][...] = out.astype(BF16)


def _post(ins, res, kv, params, *, S, tm, emit_bf16):
    (w_mix, g1, b1, wq, wo, g2, b2, w_in, cw, cb, w_out, g3, b3) = params
    T = res.shape[0]
    tiles_per_seq = S // tm
    kern = functools.partial(_post_kernel, n_in=len(ins), tm=tm, tiles_per_seq=tiles_per_seq, emit_bf16=emit_bf16)
    row_tile = pl.BlockSpec((tm, D_MODEL), lambda i: (i, 0))
    vec = pl.BlockSpec((1, D_MODEL), lambda i: (0, 0))
    in_specs = [pl.BlockSpec((tm, a.shape[1]), lambda i: (i, 0)) for a in ins]
    in_specs += [
        _const_spec(w_mix.shape), row_tile, vec, vec,
        pl.BlockSpec((N_MEM, 2 * D_MODEL), lambda i: (i // tiles_per_seq, 0)),
        _const_spec(wq.shape), _const_spec(wo.shape), vec, vec,
        _const_spec(w_in.shape), _const_spec(cw.shape), _const_spec(cb.shape), _const_spec(w_out.shape), vec, vec,
    ]
    out_specs = [row_tile]
    out_shape = [jax.ShapeDtypeStruct((T, D_MODEL), F32)]
    if emit_bf16:
        out_specs.append(row_tile)
        out_shape.append(jax.ShapeDtypeStruct((T, D_MODEL), BF16))
    return pl.pallas_call(
        kern,
        grid=(T // tm,),
        in_specs=in_specs,
        out_specs=out_specs,
        out_shape=out_shape,
        scratch_shapes=[
            pltpu.VMEM((tm, D_MODEL), BF16),
            pltpu.VMEM((tm + HALO, D_MODEL), BF16),
            pltpu.VMEM((tm + HALO, FF_CHUNK), F32),
            pltpu.VMEM((tm, D_FF), BF16),
        ],
        compiler_params=_cparams(("arbitrary",)),
        name="post",
    )(*ins, w_mix, res, g1, b1, kv, wq, wo, g2, b2, w_in, cw, cb, w_out, g3, b3)


def _rope_consts():
    half = ROPE_DIM // 2
    inv_freq = ROPE_THETA ** (-jnp.arange(0, ROPE_DIM, 2, dtype=F32) / ROPE_DIM)
    d = jnp.arange(LANES) % D_A
    sel = ((d[None, :] < ROPE_DIM) & (d[None, :] % half == jnp.arange(half)[:, None])).astype(F32)
    zero = jnp.zeros((half, LANES), F32)
    expand = jnp.concatenate([jnp.concatenate([sel, zero], axis=1),
                              jnp.concatenate([zero, sel], axis=1)], axis=0).astype(BF16)
    off = jnp.where(d >= ROPE_DIM, 1.0, 0.0)
    lo = jnp.where(d < half, -1.0, 0.0)
    hi = jnp.where((d >= half) & (d < ROPE_DIM), 1.0, 0.0)
    lane_c = jnp.zeros((8, LANES), F32).at[0].set(off).at[1].set(lo).at[2].set(hi)
    return inv_freq.reshape(half, 1), expand, lane_c


def _tile_sizes(S):
    tm = min(512, S)
    tq = min(128, S)
    rb = min(1024, S)
    assert S % tm == 0 and S % tq == 0 and S % rb == 0 and rb % (4 * GLA_GROUP) == 0
    return tm, tq, rb


def _row(v):
    return v.reshape(1, -1).astype(F32)


def _tail_w(w, n_main):
    tail = w[:, n_main:]
    return jnp.pad(tail, ((0, 0), (0, LANES - tail.shape[1]))).astype(BF16)


def _post_layer(ins, res, mem2d, S, params, tm, emit_bf16):
    (w_mix, ln1_g, ln1_b, xa_wq, xa_wkv, xa_wo, ln2_g, ln2_b, f_in, f_cw, f_cb, f_out, ln3_g, ln3_b) = params
    kv = _kvproj(mem2d, xa_wkv.astype(BF16), tm=tm)
    prepared = (w_mix.astype(BF16), _row(ln1_g), _row(ln1_b), xa_wq.astype(BF16), xa_wo.astype(BF16),
                _row(ln2_g), _row(ln2_b), f_in.astype(BF16), f_cw.astype(F32), _row(f_cb), f_out.astype(BF16),
                _row(ln3_g), _row(ln3_b))
    return _post(ins, res, kv, prepared, S=S, tm=tm, emit_bf16=emit_bf16)


def kernel(x, mem, positions, w_in_0, diff_lambda_0, diff_norm_0, gla_w2_0, gla_b2_0, gla_norm_0, w_mix_out_0, ln1_g_0, ln1_b_0, xa_wq_0, xa_wkv_0, xa_wo_0, ln2_g_0, ln2_b_0, ffn_w_in_0, ffn_conv_w_0, ffn_conv_b_0, ffn_w_out_0, ln3_g_0, ln3_b_0, w_in_1, gdn_conv_w_1, gdn_a_log_1, gdn_dt_bias_1, gdn_norm_1, w_mix_out_1, ln1_g_1, ln1_b_1, xa_wq_1, xa_wkv_1, xa_wo_1, ln2_g_1, ln2_b_1, ffn_w_in_1, ffn_conv_w_1, ffn_conv_b_1, ffn_w_out_1, ln3_g_1, ln3_b_1):
    B, S, _ = x.shape
    T = B * S
    tm, tq, rb = _tile_sizes(S)
    x2d = x.reshape(T, D_MODEL)
    mem2d = mem.reshape(B * N_MEM, D_MODEL)

    hp, tail = _proj0(x2d, positions.reshape(T // tm, 1, tm), _rope_consts(),
                      w_in_0[:, :P0_MAIN].astype(BF16), _tail_w(w_in_0, P0_MAIN), tm=tm)
    lam_init = 0.8 - 0.6 * math.exp(-0.3 * 0)
    oa = _diffattn(hp, diff_lambda_0.astype(F32), diff_norm_0.reshape(DV_A, 1).astype(F32), B=B, S=S,
                   tq=tq, lam_init=lam_init)
    w2p = jnp.pad(gla_w2_0, ((0, LANES - GATE_RANK), (0, 0))).astype(BF16)
    ob = _gla(hp, tail, w2p, _row(gla_b2_0), _row(gla_norm_0), B=B, S=S, rb=rb)
    h, hb = _post_layer([oa, ob], x2d, mem2d, S,
                        (w_mix_out_0, ln1_g_0, ln1_b_0, xa_wq_0, xa_wkv_0, xa_wo_0, ln2_g_0, ln2_b_0,
                         ffn_w_in_0, ffn_conv_w_0, ffn_conv_b_0, ffn_w_out_0, ln3_g_0, ln3_b_0), tm, True)

    hp, tail = _proj1(hb, w_in_1[:, :P1_MAIN].astype(BF16), _tail_w(w_in_1, P1_MAIN), gdn_conv_w_1.astype(F32),
                      S=S, tm=tm)
    head_par = (jnp.zeros((8, LANES), F32)
                .at[0, H_C:2 * H_C].set(gdn_a_log_1.astype(F32))
                .at[1, H_C:2 * H_C].set(gdn_dt_bias_1.astype(F32)))
    oc = _gdn(hp, tail, head_par, _row(gdn_norm_1), B=B, S=S, rb=rb)
    (out,) = _post_layer([oc], h, mem2d, S,
                         (w_mix_out_1, ln1_g_1, ln1_b_1, xa_wq_1, xa_wkv_1, xa_wo_1, ln2_g_1, ln2_b_1,
                          ffn_w_in_1, ffn_conv_w_1, ffn_conv_b_1, ffn_w_out_1, ln3_g_1, ln3_b_1), tm, False)
    return out.reshape(B, S, D_MODEL)
```

```python
import functools
import math

import jax
import jax.numpy as jnp
from jax import lax
from jax.experimental import pallas as pl
from jax.experimental.pallas import tpu as pltpu

F32 = jnp.float32
BF16 = jnp.bfloat16

D_MODEL = 1024
N_MEM = 256
H_A, D_A, DV_A = 4, 64, 128
ROPE_THETA, ROPE_DIM = 500000.0, 16
H_B, DK_B, DV_B = 4, 64, 128
GATE_RANK, GATE_NORMALIZER = 16, 16.0
H_C, DK_C, DV_C, CONV_C = 8, 128, 128, 4
CHUNK = 64
N_XA, D_XA = 4, 256
D_FF = 2816
FFN_CONV = 3
LN_EPS, RMS_EPS = 1e-5, 1e-6
DEPTH = 2
ALPHA = (2.0 * DEPTH) ** 0.25
P0_MAIN = 3072
P1_MAIN = 4096
LANES = 128
FF_CHUNK = 256
N_FF_CHUNKS = D_FF // FF_CHUNK
HALO = 16
VMEM_LIMIT = 56 * 1024 * 1024

NT_DIMS = (((1,), (1,)), ((), ()))
TN_DIMS = (((0,), (0,)), ((), ()))
HI = lax.Precision.HIGHEST


def _cparams(sem):
    return pltpu.CompilerParams(dimension_semantics=sem, vmem_limit_bytes=VMEM_LIMIT)


def _const_spec(shape):
    nd = len(shape)
    return pl.BlockSpec(shape, lambda *_: (0,) * nd, pipeline_mode=pl.Buffered(1))


def _dot(a, b):
    return jnp.dot(a, b, preferred_element_type=F32)


def _dot_nt(a, b, precision=None):
    return lax.dot_general(a, b, NT_DIMS, precision=precision, preferred_element_type=F32)


def _layer_norm(y, g, b):
    mu = jnp.mean(y, axis=-1, keepdims=True)
    yc = y - mu
    var = jnp.mean(yc * yc, axis=-1, keepdims=True)
    return yc * lax.rsqrt(var + LN_EPS) * g + b


def _sigmoid(x):
    return 1.0 / (1.0 + jnp.exp(-x))


def _softplus(x):
    return jnp.maximum(x, 0.0) + jnp.log(1.0 + jnp.exp(-jnp.abs(x)))


def _tril_mask(n, strict=False):
    r = lax.broadcasted_iota(jnp.int32, (n, n), 0)
    c = lax.broadcasted_iota(jnp.int32, (n, n), 1)
    return (c < r) if strict else (c <= r)


def _proj0_kernel(x_ref, pos_ref, freq_ref, expand_ref, rc_ref, w_ref, wt_ref, o_ref, t_ref):
    xb = x_ref[...].astype(BF16)
    ang_t = freq_ref[...] * pos_ref[0].astype(F32)
    trig_t = jnp.concatenate([jnp.cos(ang_t), jnp.sin(ang_t)], axis=0)
    hi = trig_t.astype(BF16)
    lo = (trig_t - hi.astype(F32)).astype(BF16)
    e = expand_ref[...]
    pat = (lax.dot_general(hi, e, TN_DIMS, preferred_element_type=F32)
           + lax.dot_general(lo, e, TN_DIMS, preferred_element_type=F32))
    c = pat[:, 0:LANES] + rc_ref[0:1, :]
    s_lo = pat[:, LANES:2 * LANES] * rc_ref[1:2, :]
    s_hi = pat[:, LANES:2 * LANES] * rc_ref[2:3, :]
    q_scale = D_A ** -0.5
    for j in range(2):
        y = _dot(xb, w_ref[:, j * 512:(j + 1) * 512])
        for t in range(4):
            yt = y[:, t * LANES:(t + 1) * LANES]
            yt = yt * c + pltpu.roll(yt, LANES - 8, 1) * s_lo + pltpu.roll(yt, 8, 1) * s_hi
            if j == 0:
                yt = yt * q_scale
            col = j * 512 + t * LANES
            o_ref[:, col:col + LANES] = yt.astype(BF16)
    for col in range(1024, P0_MAIN, 512):
        o_ref[:, col:col + 512] = _dot(xb, w_ref[:, col:col + 512]).astype(BF16)
    t_ref[...] = _dot(xb, wt_ref[...])


def _proj0(x2d, pos_rows, rope_c, w_main, w_tail, *, tm):
    T = x2d.shape[0]
    freq, expand, lane_c = rope_c
    return pl.pallas_call(
        _proj0_kernel,
        grid=(T // tm,),
        in_specs=[
            pl.BlockSpec((tm, D_MODEL), lambda i: (i, 0)),
            pl.BlockSpec((1, 1, tm), lambda i: (i, 0, 0)),
            _const_spec(freq.shape),
            _const_spec(expand.shape),
            _const_spec(lane_c.shape),
            _const_spec((D_MODEL, P0_MAIN)),
            _const_spec((D_MODEL, LANES)),
        ],
        out_specs=[
            pl.BlockSpec((tm, P0_MAIN), lambda i: (i, 0)),
            pl.BlockSpec((tm, LANES), lambda i: (i, 0)),
        ],
        out_shape=[
            jax.ShapeDtypeStruct((T, P0_MAIN), BF16),
            jax.ShapeDtypeStruct((T, LANES), F32),
        ],
        compiler_params=_cparams(("parallel",)),
        name="proj0",
    )(x2d, pos_rows, freq, expand, lane_c, w_main, w_tail)


QK_C = 2 * H_C * DK_C
PROJ_CHUNK = 512


def _proj1_kernel(x_ref, w_ref, wt_ref, cw_ref, o_ref, t_ref, ybuf_sc, carry_sc, *, tm, tiles_per_seq):
    xb = x_ref[...]

    @pl.when((pl.program_id(0) % tiles_per_seq) == 0)
    def _():
        carry_sc[...] = jnp.zeros(carry_sc.shape, F32)

    def plain(col):
        o_ref[:, col:col + PROJ_CHUNK] = _dot(xb, w_ref[:, col:col + PROJ_CHUNK]).astype(BF16)

    def tail():
        t_ref[...] = _dot(xb, wt_ref[...])

    fillers = [functools.partial(plain, col) for col in range(QK_C, P1_MAIN, PROJ_CHUNK)] + [tail]
    for ci, col in enumerate(range(0, QK_C, PROJ_CHUNK)):
        ybuf_sc[0:8, :] = carry_sc[ci]
        ybuf_sc[8:8 + tm, :] = _dot(xb, w_ref[:, col:col + PROJ_CHUNK])
        carry_sc[ci] = ybuf_sc[tm:tm + 8, :]
        if fillers:
            fillers.pop(0)()
        w = cw_ref[:, col:col + PROJ_CHUNK]
        y = w[CONV_C - 1:CONV_C, :] * ybuf_sc[8:8 + tm, :]
        for j in range(CONV_C - 1):
            y = y + w[j:j + 1, :] * ybuf_sc[5 + j:5 + j + tm, :]
        y = y * _sigmoid(y)
        scale = DK_C ** -0.5 if col < H_C * DK_C else 1.0
        for t in range(PROJ_CHUNK // LANES):
            yt = y[:, t * LANES:(t + 1) * LANES]
            yt = yt * (lax.rsqrt(jnp.sum(yt * yt, axis=-1, keepdims=True) + RMS_EPS) * scale)
            o_ref[:, col + t * LANES:col + (t + 1) * LANES] = yt.astype(BF16)
    for f in fillers:
        f()


def _proj1(xb2d, w_main, w_tail, conv_w, *, S, tm):
    T = xb2d.shape[0]
    kern = functools.partial(_proj1_kernel, tm=tm, tiles_per_seq=S // tm)
    return pl.pallas_call(
        kern,
        grid=(T // tm,),
        in_specs=[
            pl.BlockSpec((tm, D_MODEL), lambda i: (i, 0)),
            _const_spec((D_MODEL, P1_MAIN)),
            _const_spec((D_MODEL, LANES)),
            _const_spec((CONV_C, QK_C)),
        ],
        out_specs=[
            pl.BlockSpec((tm, P1_MAIN), lambda i: (i, 0)),
            pl.BlockSpec((tm, LANES), lambda i: (i, 0)),
        ],
        out_shape=[
            jax.ShapeDtypeStruct((T, P1_MAIN), BF16),
            jax.ShapeDtypeStruct((T, LANES), F32),
        ],
        scratch_shapes=[
            pltpu.VMEM((tm + 8, PROJ_CHUNK), F32),
            pltpu.VMEM((QK_C // PROJ_CHUNK, 8, PROJ_CHUNK), F32),
        ],
        compiler_params=_cparams(("arbitrary",)),
        name="proj1",
    )(xb2d, w_main, w_tail, conv_w)


def _kvproj_kernel(m_ref, w_ref, o_ref):
    mb = m_ref[...].astype(BF16)
    for col in range(0, 2 * D_MODEL, 512):
        o_ref[:, col:col + 512] = _dot(mb, w_ref[:, col:col + 512]).astype(BF16)


def _kvproj(mem2d, wkv, *, tm):
    R = mem2d.shape[0]
    return pl.pallas_call(
        _kvproj_kernel,
        grid=(R // tm,),
        in_specs=[
            pl.BlockSpec((tm, D_MODEL), lambda i: (i, 0)),
            _const_spec((D_MODEL, 2 * D_MODEL)),
        ],
        out_specs=pl.BlockSpec((tm, 2 * D_MODEL), lambda i: (i, 0)),
        out_shape=jax.ShapeDtypeStruct((R, 2 * D_MODEL), BF16),
        compiler_params=_cparams(("parallel",)),
        name="kvproj",
    )(mem2d, wkv)


def _diffattn_kernel(q_ref, k_ref, v_ref, lam_ref, g_ref, o_ref, *, S, tq, lam_init):
    lv = lam_ref[...]
    lam = (jnp.exp(jnp.sum(lv[0:1] * lv[1:2], axis=-1, keepdims=True))
           - jnp.exp(jnp.sum(lv[2:3] * lv[3:4], axis=-1, keepdims=True)) + lam_init)
    lane = lax.broadcasted_iota(jnp.int32, (tq, LANES), 1)
    r = lax.broadcasted_iota(jnp.int32, (2 * tq, tq), 0)
    c = lax.broadcasted_iota(jnp.int32, (2 * tq, tq), 1)
    causal = c <= jnp.where(r >= tq, r - tq, r)
    vt = v_ref[...].astype(F32).T
    vt_ext = jnp.concatenate([vt, jnp.ones((HALO, S), F32)], axis=0).astype(BF16)
    g_col = g_ref[...]
    def scores(qi):
        n_keys = (qi + 1) * tq
        q = q_ref[qi * tq:(qi + 1) * tq, :]
        zero = jnp.zeros_like(q)
        qq = jnp.concatenate([jnp.where(lane < D_A, q, zero), jnp.where(lane >= D_A, q, zero)], axis=0)
        s = _dot_nt(qq, k_ref[0:n_keys, :])
        s_diag = jnp.where(causal, s[:, n_keys - tq:], -jnp.inf)
        return s_diag if qi == 0 else jnp.concatenate([s[:, :n_keys - tq], s_diag], axis=1)

    def softmax_num(s):
        return jnp.exp(s - jnp.max(s, axis=-1, keepdims=True)).astype(BF16)

    def finish(qi, p):
        n_keys = (qi + 1) * tq
        ot = _dot_nt(vt_ext[:, 0:n_keys], p)
        o_all = ot[:DV_A] / ot[DV_A:DV_A + 1]
        o = o_all[:, :tq] - lam * o_all[:, tq:]
        ms = jnp.mean(o * o, axis=0, keepdims=True)
        y = o * lax.rsqrt(ms + RMS_EPS) * g_col * (1.0 - lam_init)
        o_ref[qi * tq:(qi + 1) * tq, :] = y.T.astype(BF16)

    nq = S // tq
    pairs = [(i, nq - 1 - i) for i in range(nq // 2)] + ([(nq // 2,)] if nq % 2 else [])
    order = [sum(pairs[i:i + 2], ()) for i in range(0, len(pairs), 2)]
    for group in order:
        ss = [scores(qi) for qi in group]
        ps = [softmax_num(s) for s in ss]
        for qi, p in zip(group, ps):
            finish(qi, p)


def _diffattn(hp, lam_vecs, norm_g, *, B, S, tq, lam_init):
    T = B * S
    kern = functools.partial(_diffattn_kernel, S=S, tq=tq, lam_init=lam_init)
    return pl.pallas_call(
        kern,
        grid=(B, H_A),
        in_specs=[
            pl.BlockSpec((S, LANES), lambda b, h: (b, h)),
            pl.BlockSpec((S, LANES), lambda b, h: (b, H_A + h)),
            pl.BlockSpec((S, LANES), lambda b, h: (b, 2 * H_A + h)),
            pl.BlockSpec((4, D_A), lambda b, h: (0, 0)),
            pl.BlockSpec((DV_A, 1), lambda b, h: (0, 0)),
        ],
        out_specs=pl.BlockSpec((S, LANES), lambda b, h: (b, h)),
        out_shape=jax.ShapeDtypeStruct((T, H_A * DV_A), BF16),
        compiler_params=_cparams(("parallel", "parallel")),
        name="diffattn",
    )(hp, hp, hp, lam_vecs, norm_g)


GLA_GROUP = 2 * CHUNK
GLA_LEVELS = (32, 16, 8, 4, 2, 1)


def _split_dot(m_bf16, hi, lo):
    return _dot(m_bf16, hi) + _dot(m_bf16, lo)


def _gla_kernel(q_ref, k_ref, v_ref, r_ref, t_ref, w2_ref, b2_ref, g_ref, o_ref, st_sc, *, rb):
    G = GLA_GROUP

    @pl.when(pl.program_id(1) == 0)
    def _():
        st_sc[...] = jnp.zeros(st_sc.shape, F32)

    r = lax.broadcasted_iota(jnp.int32, (G, G), 0)
    c = lax.broadcasted_iota(jnp.int32, (G, G), 1)
    tril2 = jnp.where(((r >= CHUNK) == (c >= CHUNK)) & (c <= r), 1.0, 0.0).astype(BF16)
    row = lax.broadcasted_iota(jnp.int32, (G, H_B * DK_B), 0)
    lane = lax.broadcasted_iota(jnp.int32, (G, LANES), 1)
    head_lanes = [lane < DK_B, lane >= DK_B]
    lane_c = lax.broadcasted_iota(jnp.int32, (CHUNK, LANES), 1)
    head_lanes_c = [lane_c < DK_B, lane_c >= DK_B]
    levels = []
    for w in GLA_LEVELS:
        same = (r // (2 * w)) == (c // (2 * w))
        r_in_i = (r % (2 * w)) >= w
        c_in_i = (c % (2 * w)) >= w
        in_range = (r_in_i & c_in_i & (c <= r)) | ((~r_in_i) & (~c_in_i) & (c > r))
        m_arg = jnp.where(same & in_range, 1.0, 0.0).astype(BF16)
        levels.append((m_arg, same, (row % (2 * w)) >= w))
    eye = r == c

    def group(gi, carry):
        r0 = pl.multiple_of(gi * G, G)
        rows = pl.ds(r0, G)
        x = _dot(t_ref[rows, :].astype(BF16), w2_ref[...]) + b2_ref[...]
        gk = (jnp.minimum(x, 0.0) - jnp.log(1.0 + jnp.exp(-jnp.abs(x)))) * (1.0 / GATE_NORMALIZER)
        hi = gk.astype(BF16)
        lo = (gk - hi.astype(F32)).astype(BF16)
        b = _split_dot(tril2, hi, lo)
        q = q_ref[rows, :].astype(F32) * (DK_B ** -0.5)
        k = k_ref[rows, :].astype(F32)
        v = v_ref[rows, :]
        q_lv, k_lv, masks = [], [], []
        for m_arg, same, row_in_i in levels:
            e = jnp.exp(_split_dot(m_arg, hi, lo))
            q_lv.append(jnp.where(row_in_i, q * e, 0.0))
            k_lv.append(jnp.where(row_in_i, 0.0, k * e).astype(BF16))
            masks.append(same)
        q_lv.append(q)
        k_lv.append(k.astype(BF16))
        masks.append(eye)
        atts = [None] * H_B
        for ql, kl_, mask in zip(q_lv, k_lv, masks):
            for h in range(H_B):
                blk = slice((h // 2) * LANES, (h // 2 + 1) * LANES)
                qm = jnp.where(head_lanes[h % 2], ql[:, blk], 0.0).astype(BF16)
                part = jnp.where(mask, _dot_nt(qm, kl_[:, blk]), 0.0)
                atts[h] = part if atts[h] is None else atts[h] + part
        avs = [_dot(atts[h].astype(BF16), v[:, h * DV_B:(h + 1) * DV_B]) for h in range(H_B)]
        qe = q * jnp.exp(b)
        sts = [st_sc[p] for p in range(H_B // 2)]
        inters = [[] for _ in range(H_B)]
        for c2 in range(G // CHUNK):
            cs = slice(c2 * CHUNK, (c2 + 1) * CHUNK)
            for h in range(H_B):
                blk = slice((h // 2) * LANES, (h // 2 + 1) * LANES)
                qm = jnp.where(head_lanes_c[h % 2], qe[cs, blk], 0.0).astype(BF16)
                inters[h].append(_dot_nt(qm, sts[h // 2].astype(BF16)))
            bl = b[(c2 + 1) * CHUNK - 1:(c2 + 1) * CHUNK, :]
            kl = (k[cs] * jnp.exp(bl - b[cs])).astype(BF16)
            ebl = jnp.exp(bl)
            kvs = [lax.dot_general(v[cs, h * DV_B:(h + 1) * DV_B], kl[:, (h // 2) * LANES:(h // 2 + 1) * LANES],
                                   TN_DIMS, preferred_element_type=F32) for h in range(H_B)]
            sts = [sts[p] * ebl[:, p * LANES:(p + 1) * LANES]
                   + jnp.where(head_lanes[0], kvs[2 * p], kvs[2 * p + 1]) for p in range(H_B // 2)]
        for p in range(H_B // 2):
            st_sc[p] = sts[p]
        for h in range(H_B):
            vs = slice(h * DV_B, (h + 1) * DV_B)
            o = jnp.concatenate(inters[h], axis=0) + avs[h]
            ms = jnp.mean(o * o, axis=-1, keepdims=True)
            gate = r_ref[rows, vs].astype(F32)
            y = o * lax.rsqrt(ms + RMS_EPS) * g_ref[...] * (gate * _sigmoid(gate))
            o_ref[rows, vs] = y.astype(BF16)
        return carry

    lax.fori_loop(0, rb // G, group, 0, unroll=4)


def _gla(hp, tail, w2p, b2, norm_g, *, B, S, rb):
    nb = S // rb
    T = B * S
    kern = functools.partial(_gla_kernel, rb=rb)
    return pl.pallas_call(
        kern,
        grid=(B, nb),
        in_specs=[
            pl.BlockSpec((rb, H_B * DK_B), lambda b, i: (b * nb + i, 6)),
            pl.BlockSpec((rb, H_B * DK_B), lambda b, i: (b * nb + i, 7)),
            pl.BlockSpec((rb, H_B * DV_B), lambda b, i: (b * nb + i, 4)),
            pl.BlockSpec((rb, H_B * DV_B), lambda b, i: (b * nb + i, 5)),
            pl.BlockSpec((rb, LANES), lambda b, i: (b * nb + i, 0)),
            pl.BlockSpec((LANES, H_B * DK_B), lambda b, i: (0, 0)),
            pl.BlockSpec((1, H_B * DK_B), lambda b, i: (0, 0)),
            pl.BlockSpec((1, DV_B), lambda b, i: (0, 0)),
        ],
        out_specs=pl.BlockSpec((rb, H_B * DV_B), lambda b, i: (b * nb + i, 0)),
        out_shape=jax.ShapeDtypeStruct((T, H_B * DV_B), BF16),
        scratch_shapes=[pltpu.VMEM((H_B // 2, DV_B, 2 * DK_B), F32)],
        compiler_params=_cparams(("parallel", "arbitrary")),
        name="gla",
    )(hp, hp, hp, hp, tail, w2p, b2, norm_g)


GDN_GROUP = 2 * CHUNK


def _gdn_kernel(q_ref, k_ref, v_ref, z_ref, t_ref, cw_ref, hp_ref, g_ref, o_ref, s_sc, vwin_sc, *, rb):
    G = GDN_GROUP

    @pl.when(pl.program_id(1) == 0)
    def _():
        s_sc[...] = jnp.zeros(s_sc.shape, F32)
        vwin_sc[0:8, :] = jnp.zeros((8, H_C * DV_C), F32)

    def v_conv_silu(h):
        ls = slice(h * LANES, (h + 1) * LANES)
        w = cw_ref[:, ls]
        y = w[CONV_C - 1:CONV_C, :] * vwin_sc[8:8 + G, ls]
        for j in range(CONV_C - 1):
            y = y + w[j:j + 1, :] * vwin_sc[5 + j:5 + j + G, ls]
        return y * _sigmoid(y)

    r = lax.broadcasted_iota(jnp.int32, (G, G), 0)
    c = lax.broadcasted_iota(jnp.int32, (G, G), 1)
    same = (r >= CHUNK) == (c >= CHUNK)
    incl = same & (c <= r)
    strict = same & (c < r)
    tril2 = jnp.where(incl, 1.0, 0.0).astype(BF16)
    eye = (r == c).astype(F32)

    def group(gi, carry):
        r0 = pl.multiple_of(gi * G, G)
        rows = pl.ds(r0, G)
        t = t_ref[rows, :]
        beta_all = _sigmoid(t)
        g_all = -jnp.exp(hp_ref[0:1, :]) * _softplus(t + hp_ref[1:2, :])
        g_hi = g_all.astype(BF16)
        g_rest = g_all - g_hi.astype(F32)
        g_mid = g_rest.astype(BF16)
        g_lo = (g_rest - g_mid.astype(F32)).astype(BF16)
        cum_all = _dot(tril2, g_hi) + _dot(tril2, g_mid) + _dot(tril2, g_lo)
        cum_all_t = cum_all.T
        heads = range(H_C)
        khbs = [k_ref[rows, h * LANES:(h + 1) * LANES] for h in heads]
        qhbs = [q_ref[rows, h * LANES:(h + 1) * LANES] for h in heads]
        qs = [q.astype(F32) for q in qhbs]
        ks = [k.astype(F32) for k in khbs]
        vwin_sc[8:8 + G, :] = v_ref[rows, :].astype(F32)
        vs = [v_conv_silu(h) for h in heads]
        vwin_sc[0:8, :] = vwin_sc[G:G + 8, :]
        bhs = [jnp.broadcast_to(beta_all[:, h:h + 1], (G, LANES)) for h in heads]
        chs = [jnp.broadcast_to(cum_all[:, H_C + h:H_C + h + 1], (G, LANES)) for h in heads]
        crows = [jnp.broadcast_to(cum_all_t[H_C + h:H_C + h + 1, :], (G, G)) for h in heads]
        decs = [jnp.where(incl, jnp.exp(ch - crow), 0.0) for ch, crow in zip(chs, crows)]
        egs = [jnp.exp(ch) for ch in chs]
        kbs = [k * bh for k, bh in zip(ks, bhs)]
        kks = [_dot_nt(kb.astype(BF16), khb) for kb, khb in zip(kbs, khbs)]
        ps = [-jnp.where(strict, kk * dec, 0.0) for kk, dec in zip(kks, decs)]
        tinvs = [eye + p for p in ps]
        for _ in range(5):
            pbs = [p.astype(BF16) for p in ps]
            ps = [_dot(pb, pb) for pb in pbs]
            tinvs = [tinv + _dot(p.astype(BF16), tinv.astype(BF16)) for p, tinv in zip(ps, tinvs)]
        rhss = [jnp.concatenate([v * bh, kb * eg], axis=1).astype(BF16)
                for v, bh, kb, eg in zip(vs, bhs, kbs, egs)]
        uws = [_dot(tinv.astype(BF16), rhs) for tinv, rhs in zip(tinvs, rhss)]
        qks = [_dot_nt(qb, khb) * dec for qb, khb, dec in zip(qhbs, khbs, decs)]
        qgs = [q * eg for q, eg in zip(qs, egs)]
        sts = [s_sc[h] for h in heads]
        outs = [[] for _ in heads]
        for c2 in range(G // CHUNK):
            cs = slice(c2 * CHUNK, (c2 + 1) * CHUNK)
            wss = [_dot(jnp.concatenate([uw[cs, DV_C:], qg[cs]], axis=0).astype(BF16), st.astype(BF16))
                   for uw, qg, st in zip(uws, qgs, sts)]
            v_news = [uw[cs, :DV_C] - ws[:CHUNK] for uw, ws in zip(uws, wss)]
            for h in heads:
                outs[h].append(wss[h][CHUNK:] + _dot(qks[h][cs, cs].astype(BF16), v_news[h].astype(BF16)))
            gls = [ch[(c2 + 1) * CHUNK - 1:(c2 + 1) * CHUNK, :] for ch in chs]
            kgs = [k[cs] * jnp.exp(gl - ch[cs]) for k, gl, ch in zip(ks, gls, chs)]
            sts = [st * jnp.exp(gl) + _dot(kg.T.astype(BF16), v_new.astype(BF16))
                   for st, gl, kg, v_new in zip(sts, gls, kgs, v_news)]
        for h in heads:
            ls = slice(h * LANES, (h + 1) * LANES)
            s_sc[h] = sts[h]
            o = jnp.concatenate(outs[h], axis=0)
            ms = jnp.mean(o * o, axis=-1, keepdims=True)
            z = z_ref[pl.ds(r0, G), ls].astype(F32)
            y = o * lax.rsqrt(ms + RMS_EPS) * g_ref[...] * (z * _sigmoid(z))
            o_ref[pl.ds(r0, G), ls] = y.astype(BF16)
        return carry

    lax.fori_loop(0, rb // G, group, 0, unroll=2)


def _gdn(hp, tail, conv_w_v, head_par, norm_g, *, B, S, rb):
    nb = S // rb
    T = B * S
    width = H_C * DK_C
    kern = functools.partial(_gdn_kernel, rb=rb)
    return pl.pallas_call(
        kern,
        grid=(B, nb),
        in_specs=[
            pl.BlockSpec((rb, width), lambda b, i: (b * nb + i, 0)),
            pl.BlockSpec((rb, width), lambda b, i: (b * nb + i, 1)),
            pl.BlockSpec((rb, width), lambda b, i: (b * nb + i, 2)),
            pl.BlockSpec((rb, width), lambda b, i: (b * nb + i, 3)),
            pl.BlockSpec((rb, LANES), lambda b, i: (b * nb + i, 0)),
            pl.BlockSpec((CONV_C, width), lambda b, i: (0, 0)),
            pl.BlockSpec((8, LANES), lambda b, i: (0, 0)),
            pl.BlockSpec((1, DV_C), lambda b, i: (0, 0)),
        ],
        out_specs=pl.BlockSpec((rb, width), lambda b, i: (b * nb + i, 0)),
        out_shape=jax.ShapeDtypeStruct((T, width), BF16),
        scratch_shapes=[
            pltpu.VMEM((H_C, DK_C, DV_C), F32),
            pltpu.VMEM((GDN_GROUP + 8, width), F32),
        ],
        compiler_params=_cparams(("parallel", "arbitrary")),
        name="gdn",
    )(hp, hp, hp, hp, tail, conv_w_v, head_par, norm_g)


N_POST_PARAMS = 15


def _post_kernel(*refs, n_in, tm, tiles_per_seq, emit_bf16):
    ins = refs[:n_in]
    (wmix_ref, res_ref, g1_ref, b1_ref, kv_ref, wq_ref, wo_ref, g2_ref, b2_ref,
     win_ref, cw_ref, cb_ref, wout_ref, g3_ref, b3_ref) = refs[n_in:n_in + N_POST_PARAMS]
    n_out = 2 if emit_bf16 else 1
    outs = refs[n_in + N_POST_PARAMS:n_in + N_POST_PARAMS + n_out]
    att_sc, xcat_sc, hid_sc, act_sc = refs[n_in + N_POST_PARAMS + n_out:]

    y = ALPHA * res_ref[...]
    off = 0
    for r in ins:
        kdim = r.shape[1]
        y = y + _dot(r[...], wmix_ref[off:off + kdim, :])
        off += kdim
    h1 = _layer_norm(y, g1_ref[...], b1_ref[...])

    q = (_dot(h1.astype(BF16), wq_ref[...]) * (D_XA ** -0.5)).astype(BF16)
    for hd in range(N_XA):
        cs = slice(hd * D_XA, (hd + 1) * D_XA)
        s = _dot_nt(q[:, cs], kv_ref[:, cs])
        p = jnp.exp(s - jnp.max(s, axis=-1, keepdims=True))
        l = jnp.sum(p, axis=-1, keepdims=True)
        vs = slice(D_MODEL + hd * D_XA, D_MODEL + (hd + 1) * D_XA)
        att_sc[:, cs] = (_dot(p.astype(BF16), kv_ref[:, vs]) / l).astype(BF16)
    h2 = _layer_norm(ALPHA * h1 + _dot(att_sc[...], wo_ref[...]), g2_ref[...], b2_ref[...])

    is_start = (pl.program_id(0) % tiles_per_seq) == 0

    @pl.when(is_start)
    def _():
        xcat_sc[0:HALO, :] = jnp.zeros((HALO, D_MODEL), BF16)

    @pl.when(jnp.logical_not(is_start))
    def _():
        xcat_sc[0:HALO, :] = xcat_sc[tm:tm + HALO, :]

    xcat_sc[HALO:HALO + tm, :] = h2.astype(BF16)
    xc = xcat_sc[...]

    def conv(col):
        hid_sc[...] = _dot(xc, win_ref[:, col:col + FF_CHUNK])
        w = cw_ref[:, col:col + FF_CHUNK]
        y = cb_ref[:, col:col + FF_CHUNK] + w[FFN_CONV - 1:FFN_CONV, :] * hid_sc[HALO:HALO + tm, :]
        for j in range(FFN_CONV - 1):
            o = HALO - (FFN_CONV - 1) + j
            y = y + w[j:j + 1, :] * hid_sc[o:o + tm, :]
        return y

    for c in range(N_FF_CHUNKS):
        gate = conv(c * FF_CHUNK)
        up = conv(D_FF + c * FF_CHUNK)
        act_sc[:, c * FF_CHUNK:(c + 1) * FF_CHUNK] = (gate * _sigmoid(gate) * up).astype(BF16)
    out = _layer_norm(ALPHA * h2 + _dot(act_sc[...], wout_ref[...]), g3_ref[...], b3_ref[...])
    outs[0][...] = out
    if emit_bf16:
        outs[1][...] = out.astype(BF16)


def _post(ins, res, kv, params, *, S, tm, emit_bf16):
    (w_mix, g1, b1, wq, wo, g2, b2, w_in, cw, cb, w_out, g3, b3) = params
    T = res.shape[0]
    tiles_per_seq = S // tm
    kern = functools.partial(_post_kernel, n_in=len(ins), tm=tm, tiles_per_seq=tiles_per_seq, emit_bf16=emit_bf16)
    row_tile = pl.BlockSpec((tm, D_MODEL), lambda i: (i, 0))
    vec = pl.BlockSpec((1, D_MODEL), lambda i: (0, 0))
    in_specs = [pl.BlockSpec((tm, a.shape[1]), lambda i: (i, 0)) for a in ins]
    in_specs += [
        _const_spec(w_mix.shape), row_tile, vec, vec,
        pl.BlockSpec((N_MEM, 2 * D_MODEL), lambda i: (i // tiles_per_seq, 0)),
        _const_spec(wq.shape), _const_spec(wo.shape), vec, vec,
        _const_spec(w_in.shape), _const_spec(cw.shape), _const_spec(cb.shape), _const_spec(w_out.shape), vec, vec,
    ]
    out_specs = [row_tile]
    out_shape = [jax.ShapeDtypeStruct((T, D_MODEL), F32)]
    if emit_bf16:
        out_specs.append(row_tile)
        out_shape.append(jax.ShapeDtypeStruct((T, D_MODEL), BF16))
    return pl.pallas_call(
        kern,
        grid=(T // tm,),
        in_specs=in_specs,
        out_specs=out_specs,
        out_shape=out_shape,
        scratch_shapes=[
            pltpu.VMEM((tm, D_MODEL), BF16),
            pltpu.VMEM((tm + HALO, D_MODEL), BF16),
            pltpu.VMEM((tm + HALO, FF_CHUNK), F32),
            pltpu.VMEM((tm, D_FF), BF16),
        ],
        compiler_params=_cparams(("arbitrary",)),
        name="post",
    )(*ins, w_mix, res, g1, b1, kv, wq, wo, g2, b2, w_in, cw, cb, w_out, g3, b3)


def _rope_consts():
    half = ROPE_DIM // 2
    inv_freq = ROPE_THETA ** (-jnp.arange(0, ROPE_DIM, 2, dtype=F32) / ROPE_DIM)
    d = jnp.arange(LANES) % D_A
    sel = ((d[None, :] < ROPE_DIM) & (d[None, :] % half == jnp.arange(half)[:, None])).astype(F32)
    zero = jnp.zeros((half, LANES), F32)
    expand = jnp.concatenate([jnp.concatenate([sel, zero], axis=1),
                              jnp.concatenate([zero, sel], axis=1)], axis=0).astype(BF16)
    off = jnp.where(d >= ROPE_DIM, 1.0, 0.0)
    lo = jnp.where(d < half, -1.0, 0.0)
    hi = jnp.where((d >= half) & (d < ROPE_DIM), 1.0, 0.0)
    lane_c = jnp.zeros((8, LANES), F32).at[0].set(off).at[1].set(lo).at[2].set(hi)
    return inv_freq.reshape(half, 1), expand, lane_c


def _tile_sizes(S):
    tm = min(512, S)
    tq = min(128, S)
    rb = min(1024, S)
    assert S % tm == 0 and S % tq == 0 and S % rb == 0 and rb % (4 * GLA_GROUP) == 0
    return tm, tq, rb


def _row(v):
    return v.reshape(1, -1).astype(F32)


def _tail_w(w, n_main):
    tail = w[:, n_main:]
    return jnp.pad(tail, ((0, 0), (0, LANES - tail.shape[1]))).astype(BF16)


def _post_layer(ins, res, mem2d, S, params, tm, emit_bf16):
    (w_mix, ln1_g, ln1_b, xa_wq, xa_wkv, xa_wo, ln2_g, ln2_b, f_in, f_cw, f_cb, f_out, ln3_g, ln3_b) = params
    kv = _kvproj(mem2d, xa_wkv.astype(BF16), tm=tm)
    prepared = (w_mix.astype(BF16), _row(ln1_g), _row(ln1_b), xa_wq.astype(BF16), xa_wo.astype(BF16),
                _row(ln2_g), _row(ln2_b), f_in.astype(BF16), f_cw.astype(F32), _row(f_cb), f_out.astype(BF16),
                _row(ln3_g), _row(ln3_b))
    return _post(ins, res, kv, prepared, S=S, tm=tm, emit_bf16=emit_bf16)


def kernel(x, mem, positions, w_in_0, diff_lambda_0, diff_norm_0, gla_w2_0, gla_b2_0, gla_norm_0, w_mix_out_0, ln1_g_0, ln1_b_0, xa_wq_0, xa_wkv_0, xa_wo_0, ln2_g_0, ln2_b_0, ffn_w_in_0, ffn_conv_w_0, ffn_conv_b_0, ffn_w_out_0, ln3_g_0, ln3_b_0, w_in_1, gdn_conv_w_1, gdn_a_log_1, gdn_dt_bias_1, gdn_norm_1, w_mix_out_1, ln1_g_1, ln1_b_1, xa_wq_1, xa_wkv_1, xa_wo_1, ln2_g_1, ln2_b_1, ffn_w_in_1, ffn_conv_w_1, ffn_conv_b_1, ffn_w_out_1, ln3_g_1, ln3_b_1):
    B, S, _ = x.shape
    T = B * S
    tm, tq, rb = _tile_sizes(S)
    x2d = x.reshape(T, D_MODEL)
    mem2d = mem.reshape(B * N_MEM, D_MODEL)

    hp, tail = _proj0(x2d, positions.reshape(T // tm, 1, tm), _rope_consts(),
                      w_in_0[:, :P0_MAIN].astype(BF16), _tail_w(w_in_0, P0_MAIN), tm=tm)
    lam_init = 0.8 - 0.6 * math.exp(-0.3 * 0)
    oa = _diffattn(hp, diff_lambda_0.astype(F32), diff_norm_0.reshape(DV_A, 1).astype(F32), B=B, S=S,
                   tq=tq, lam_init=lam_init)
    w2p = jnp.pad(gla_w2_0, ((0, LANES - GATE_RANK), (0, 0))).astype(BF16)
    ob = _gla(hp, tail, w2p, _row(gla_b2_0), _row(gla_norm_0), B=B, S=S, rb=rb)
    h, hb = _post_layer([oa, ob], x2d, mem2d, S,
                        (w_mix_out_0, ln1_g_0, ln1_b_0, xa_wq_0, xa_wkv_0, xa_wo_0, ln2_g_0, ln2_b_0,
                         ffn_w_in_0, ffn_conv_w_0, ffn_conv_b_0, ffn_w_out_0, ln3_g_0, ln3_b_0), tm, True)

    conv_w = gdn_conv_w_1.astype(F32)
    hp, tail = _proj1(hb, w_in_1[:, :P1_MAIN].astype(BF16), _tail_w(w_in_1, P1_MAIN), conv_w[:, :QK_C], S=S, tm=tm)
    head_par = (jnp.zeros((8, LANES), F32)
                .at[0, H_C:2 * H_C].set(gdn_a_log_1.astype(F32))
                .at[1, H_C:2 * H_C].set(gdn_dt_bias_1.astype(F32)))
    oc = _gdn(hp, tail, conv_w[:, QK_C:], head_par, _row(gdn_norm_1), B=B, S=S, rb=rb)
    (out,) = _post_layer([oc], h, mem2d, S,
                         (w_mix_out_1, ln1_g_1, ln1_b_1, xa_wq_1, xa_wkv_1, xa_wo_1, ln2_g_1, ln2_b_1,
                          ffn_w_in_1, ffn_conv_w_1, ffn_conv_b_1, ffn_w_out_1, ln3_g_1, ln3_b_1), tm, False)
    return out.reshape(B, S, D_MODEL)
```
